```python
import math
import jax, jax.numpy as jnp
from jax import lax
import numpy as np

D_MODEL = 1024
BATCH = 8
SEQ = 2048
DEPTH = 2

CHUNK = 64
Q_BLOCK = 128
EPS = 1e-6
NEG = -1e30

T5_BUCKETS = 32
T5_MAX_DIST = 128

MLA_HEADS = 8
MLA_NOPE = 64
MLA_ROPE = 32
MLA_V = 64
MLA_QK = MLA_NOPE + MLA_ROPE
MLA_Q_RANK = 384
MLA_KV_RANK = 256
ROPE_THETA = 10000.0

SWA_HEADS = 8
SWA_KV_HEADS = 2
SWA_GROUP = SWA_HEADS // SWA_KV_HEADS
SWA_HEAD_DIM = 64
SWA_WINDOW = 128
SWA_BAND = SWA_WINDOW // CHUNK

CA_HEADS = 16
CA_HEAD_DIM = 64
CA_LEFT_CHUNKS = 8
CA_REL_PAST = 256
CA_REL_SIZE = CA_REL_PAST + CHUNK

PEER_HEADS = 8
PEER_N_KEYS = 128
PEER_N_EXPERTS = PEER_N_KEYS * PEER_N_KEYS
PEER_HALF = 128
PEER_TOPK = 16
PEER_TOKEN_BLOCK = 128

MLA_IN = MLA_Q_RANK + MLA_KV_RANK + MLA_ROPE
SWA_IN = (SWA_HEADS + 2 * SWA_KV_HEADS) * SWA_HEAD_DIM
EVEN_IN = MLA_IN + SWA_IN
EVEN_MIX = MLA_HEADS * MLA_V + SWA_HEADS * SWA_HEAD_DIM
ODD_IN = 3 * CA_HEADS * CA_HEAD_DIM
ODD_MIX = CA_HEADS * CA_HEAD_DIM
N_EVEN = (DEPTH + 1) // 2
N_ODD = DEPTH // 2

kernel_name = "hybrid_mla_swa_chunkattn_peer"


def rms_norm(x, g):
    xf = x.astype(jnp.float32)
    y = xf * lax.rsqrt(jnp.mean(xf * xf, axis=-1, keepdims=True) + EPS) * g.astype(jnp.float32)
    return y.astype(x.dtype)


def apply_rope(x):
    S, R = x.shape[1], x.shape[-1]
    inv = ROPE_THETA ** (-jnp.arange(0, R, 2, dtype=jnp.float32) / R)
    ang = jnp.arange(S, dtype=jnp.float32)[:, None] * inv[None, :]
    cos = jnp.cos(ang)[None, :, None, :]
    sin = jnp.sin(ang)[None, :, None, :]
    x1, x2 = jnp.split(x.astype(jnp.float32), 2, axis=-1)
    return jnp.concatenate([x1 * cos - x2 * sin, x1 * sin + x2 * cos], axis=-1).astype(x.dtype)


def t5_bucket(rel):
    nb = T5_BUCKETS // 2
    max_exact = nb // 2
    ret = jnp.where(rel > 0, nb, 0)
    n = jnp.abs(rel)
    large = max_exact + (jnp.log(jnp.maximum(n, 1).astype(jnp.float32) / max_exact)
                         / math.log(T5_MAX_DIST / max_exact) * (nb - max_exact)).astype(jnp.int32)
    large = jnp.minimum(large, nb - 1)
    return ret + jnp.where(n < max_exact, n, large)


def mla_attention(p, g_cq, w_uq, g_ckv, w_ukv, g_q, g_k):
    B, S, _ = p.shape
    c_q = rms_norm(p[..., :MLA_Q_RANK], g_cq)
    c_kv = rms_norm(p[..., MLA_Q_RANK:MLA_Q_RANK + MLA_KV_RANK], g_ckv)
    k_pe = p[..., MLA_Q_RANK + MLA_KV_RANK:]
    q = jnp.einsum('bsr,rhd->bshd', c_q, w_uq)
    kv = jnp.einsum('bsr,rhd->bshd', c_kv, w_ukv)
    k_nope, v = kv[..., :MLA_NOPE], kv[..., MLA_NOPE:]
    k = jnp.concatenate([k_nope, jnp.broadcast_to(k_pe[:, :, None, :], (B, S, MLA_HEADS, MLA_ROPE))], -1)
    q = rms_norm(q, g_q)
    k = rms_norm(k, g_k)
    q = jnp.concatenate([q[..., :MLA_NOPE], apply_rope(q[..., MLA_NOPE:])], -1)
    k = jnp.concatenate([k[..., :MLA_NOPE], apply_rope(k[..., MLA_NOPE:])], -1)
    scale = MLA_QK ** -0.5
    nqb = S // Q_BLOCK
    qb = q.reshape(B, nqb, Q_BLOCK, MLA_HEADS, MLA_QK).transpose(1, 0, 2, 3, 4)
    key_chunk = jnp.arange(S) // CHUNK

    def block(args):
        qi, idx = args
        s = jnp.einsum('bqhd,bkhd->bhqk', qi, k, preferred_element_type=jnp.float32) * scale
        q_chunk = (idx * Q_BLOCK + jnp.arange(Q_BLOCK)) // CHUNK
        mask = key_chunk[None, :] <= q_chunk[:, None]
        s = jnp.where(mask[None, None], s, NEG)
        pr = jax.nn.softmax(s, axis=-1)
        return jnp.einsum('bhqk,bkhd->bqhd', pr.astype(v.dtype), v)

    o = lax.map(block, (qb, jnp.arange(nqb)))
    return o.transpose(1, 0, 2, 3, 4).reshape(B, S, MLA_HEADS * MLA_V)


def swa_attention(p, g_q, g_k, sinks, t5_table):
    B, S, _ = p.shape
    nq = SWA_HEADS * SWA_HEAD_DIM
    nk = SWA_KV_HEADS * SWA_HEAD_DIM
    q = rms_norm(p[..., :nq].reshape(B, S, SWA_HEADS, SWA_HEAD_DIM), g_q)
    k = rms_norm(p[..., nq:nq + nk].reshape(B, S, SWA_KV_HEADS, SWA_HEAD_DIM), g_k)
    v = p[..., nq + nk:].reshape(B, S, SWA_KV_HEADS, SWA_HEAD_DIM)
    NC = S // CHUNK
    BANDW = (SWA_BAND + 1) * CHUNK
    qc = q.reshape(B, NC, CHUNK, SWA_KV_HEADS, SWA_GROUP, SWA_HEAD_DIM)
    pad = ((0, 0), (SWA_BAND, 0), (0, 0), (0, 0), (0, 0))
    kp = jnp.pad(k.reshape(B, NC, CHUNK, SWA_KV_HEADS, SWA_HEAD_DIM), pad)
    vp = jnp.pad(v.reshape(B, NC, CHUNK, SWA_KV_HEADS, SWA_HEAD_DIM), pad)
    kband = jnp.concatenate([kp[:, j:j + NC] for j in range(SWA_BAND + 1)], axis=2)
    vband = jnp.concatenate([vp[:, j:j + NC] for j in range(SWA_BAND + 1)], axis=2)
    s = jnp.einsum('bncjgd,bnmjd->bnjgcm', qc, kband,
                   preferred_element_type=jnp.float32) * (SWA_HEAD_DIM ** -0.5)
    rel = (jnp.arange(BANDW)[None, :] - SWA_BAND * CHUNK) - jnp.arange(CHUNK)[:, None]
    bias = t5_table[t5_bucket(rel)].astype(jnp.float32).transpose(2, 0, 1)
    s = s + bias.reshape(SWA_KV_HEADS, SWA_GROUP, CHUNK, BANDW)[None, None]
    key_chunk = jnp.arange(NC)[:, None] - SWA_BAND + jnp.arange(BANDW)[None, :] // CHUNK
    s = jnp.where((key_chunk >= 0)[None, :, None, None, None, :], s, NEG)
    sink = jnp.broadcast_to(sinks.astype(jnp.float32).reshape(SWA_KV_HEADS, SWA_GROUP)[None, None, :, :, None, None],
                            s.shape[:-1] + (1,))
    pr = jax.nn.softmax(jnp.concatenate([s, sink], axis=-1), axis=-1)[..., :-1]
    o = jnp.einsum('bnjgcm,bnmjd->bncjgd', pr.astype(vband.dtype), vband)
    return o.reshape(B, S, SWA_HEADS * SWA_HEAD_DIM)


def chunk_attention(p, g_q, g_k, rel_table):
    B, S, _ = p.shape
    w = CA_HEADS * CA_HEAD_DIM
    q = rms_norm(p[..., :w].reshape(B, S, CA_HEADS, CA_HEAD_DIM), g_q)
    k = rms_norm(p[..., w:2 * w].reshape(B, S, CA_HEADS, CA_HEAD_DIM), g_k)
    v = p[..., 2 * w:].reshape(B, S, CA_HEADS, CA_HEAD_DIM)
    NC = S // CHUNK
    L = CA_LEFT_CHUNKS
    BANDW = (L + 1) * CHUNK
    pad = ((0, 0), (L * CHUNK, 0), (0, 0), (0, 0))
    kp = jnp.pad(k, pad)
    vp = jnp.pad(v, pad)
    qc = q.reshape(B, NC, CHUNK, CA_HEADS, CA_HEAD_DIM).transpose(1, 0, 2, 3, 4)
    rel = (jnp.arange(BANDW)[None, :] - L * CHUNK) - jnp.arange(CHUNK)[:, None]
    idx = jnp.clip(rel, -CA_REL_PAST, CHUNK - 1) + CA_REL_PAST
    bias = rel_table[:, idx].astype(jnp.float32)
    scale = CA_HEAD_DIM ** -0.5

    def one(args):
        qi, n = args
        kb = lax.dynamic_slice_in_dim(kp, n * CHUNK, BANDW, axis=1)
        vb = lax.dynamic_slice_in_dim(vp, n * CHUNK, BANDW, axis=1)
        s = jnp.einsum('bqhd,bkhd->bhqk', qi, kb, preferred_element_type=jnp.float32) * scale + bias[None]
        valid = (n * CHUNK - L * CHUNK + jnp.arange(BANDW)) >= 0
        s = jnp.where(valid[None, None, None, :], s, NEG)
        pr = jax.nn.softmax(s, axis=-1)
        return jnp.einsum('bhqk,bkhd->bqhd', pr.astype(vb.dtype), vb)

    o = lax.map(one, (qc, jnp.arange(NC)))
    return o.transpose(1, 0, 2, 3, 4).reshape(B, S, w)


def peer(h, w_query, sub_keys, u_emb, v_emb):
    B, S, D = h.shape
    T = B * S
    xt = h.reshape(T // PEER_TOKEN_BLOCK, PEER_TOKEN_BLOCK, D)

    def block(xb):
        t = xb.shape[0]
        q = jnp.einsum('td,dhpk->thpk', xb, w_query)
        sc = jnp.einsum('thpk,hpnk->thpn', q, sub_keys, preferred_element_type=jnp.float32)
        s1, i1 = lax.top_k(sc[:, :, 0], PEER_TOPK)
        s2, i2 = lax.top_k(sc[:, :, 1], PEER_TOPK)
        cand = (s1[..., :, None] + s2[..., None, :]).reshape(t, PEER_HEADS, PEER_TOPK * PEER_TOPK)
        cand_idx = (i1[..., :, None] * PEER_N_KEYS + i2[..., None, :]).reshape(t, PEER_HEADS, PEER_TOPK * PEER_TOPK)
        top_s, pos = lax.top_k(cand, PEER_TOPK)
        eidx = jnp.take_along_axis(cand_idx, pos, axis=-1)
        g = jax.nn.softmax(top_s, axis=-1)
        u = u_emb[eidx]
        a = jax.nn.gelu(jnp.einsum('thkd,td->thk', u, xb, preferred_element_type=jnp.float32))
        vv = v_emb[eidx]
        return jnp.einsum('thk,thkd->td', (g * a).astype(vv.dtype), vv)

    return lax.map(block, xt).reshape(B, S, D)


def setup_inputs(seed: int = 0) -> dict:
    key = jax.random.key(seed)
    ks = iter(jax.random.split(key, 32))
    nrm = lambda shape, s: jax.random.normal(next(ks), shape, jnp.float32) * s
    gain = lambda shape: 1.0 + 0.02 * jax.random.normal(next(ks), shape, jnp.float32)
    return {
        "x": nrm((BATCH, SEQ, D_MODEL), 1.0),
        "t5_bias": nrm((T5_BUCKETS, SWA_HEADS), 0.5),
        "norm_mix": gain((DEPTH, D_MODEL)),
        "norm_ffn": gain((DEPTH, D_MODEL)),
        "ev_w_in": nrm((N_EVEN, D_MODEL, EVEN_IN), D_MODEL ** -0.5),
        "ev_w_out": nrm((N_EVEN, EVEN_MIX, D_MODEL), 0.5 * EVEN_MIX ** -0.5),
        "mla_g_cq": gain((N_EVEN, MLA_Q_RANK)),
        "mla_w_uq": nrm((N_EVEN, MLA_Q_RANK, MLA_HEADS, MLA_QK), MLA_Q_RANK ** -0.5),
        "mla_g_ckv": gain((N_EVEN, MLA_KV_RANK)),
        "mla_w_ukv": nrm((N_EVEN, MLA_KV_RANK, MLA_HEADS, MLA_NOPE + MLA_V), MLA_KV_RANK ** -0.5),
        "mla_g_q": gain((N_EVEN, MLA_QK)),
        "mla_g_k": gain((N_EVEN, MLA_QK)),
        "swa_g_q": gain((N_EVEN, SWA_HEAD_DIM)),
        "swa_g_k": gain((N_EVEN, SWA_HEAD_DIM)),
        "swa_sinks": nrm((N_EVEN, SWA_HEADS), 0.5),
        "od_w_in": nrm((N_ODD, D_MODEL, ODD_IN), D_MODEL ** -0.5),
        "od_w_out": nrm((N_ODD, ODD_MIX, D_MODEL), 0.5 * ODD_MIX ** -0.5),
        "ca_g_q": gain((N_ODD, CA_HEAD_DIM)),
        "ca_g_k": gain((N_ODD, CA_HEAD_DIM)),
        "ca_rel_bias": nrm((N_ODD, CA_HEADS, CA_REL_SIZE), 0.5),
        "peer_w_query": nrm((DEPTH, D_MODEL, PEER_HEADS, 2, PEER_HALF), D_MODEL ** -0.5),
        "peer_sub_keys": nrm((DEPTH, PEER_HEADS, 2, PEER_N_KEYS, PEER_HALF), PEER_HALF ** -0.5),
        "peer_u": nrm((DEPTH, PEER_N_EXPERTS, D_MODEL), D_MODEL ** -0.5),
        "peer_v": nrm((DEPTH, PEER_N_EXPERTS, D_MODEL), PEER_HEADS ** -0.5),
    }


def reference(x, t5_bias, norm_mix, norm_ffn, ev_w_in, ev_w_out, mla_g_cq, mla_w_uq, mla_g_ckv,
              mla_w_ukv, mla_g_q, mla_g_k, swa_g_q, swa_g_k, swa_sinks, od_w_in, od_w_out,
              ca_g_q, ca_g_k, ca_rel_bias, peer_w_query, peer_sub_keys, peer_u, peer_v):
    for layer in range(DEPTH):
        i = layer // 2
        h = rms_norm(x, norm_mix[layer])
        if layer % 2 == 0:
            proj = jnp.einsum('bsd,de->bse', h, ev_w_in[i])
            o_a = mla_attention(proj[..., :MLA_IN], mla_g_cq[i], mla_w_uq[i], mla_g_ckv[i],
                                mla_w_ukv[i], mla_g_q[i], mla_g_k[i])
            o_b = swa_attention(proj[..., MLA_IN:], swa_g_q[i], swa_g_k[i], swa_sinks[i], t5_bias)
            mix = jnp.einsum('bse,ed->bsd', jnp.concatenate([o_a, o_b], axis=-1), ev_w_out[i])
        else:
            proj = jnp.einsum('bsd,de->bse', h, od_w_in[i])
            o_c = chunk_attention(proj, ca_g_q[i], ca_g_k[i], ca_rel_bias[i])
            mix = jnp.einsum('bse,ed->bsd', o_c, od_w_out[i])
        x = x + mix
        x = x + peer(rms_norm(x, norm_ffn[layer]), peer_w_query[layer], peer_sub_keys[layer],
                     peer_u[layer], peer_v[layer])
    return x
```

```python
import functools
import math

import jax
import jax.numpy as jnp
import numpy as np
from jax import lax
from jax.experimental import pallas as pl
from jax.experimental.pallas import tpu as pltpu

F32 = jnp.float32
BF16 = jnp.bfloat16

D_MODEL = 1024
CHUNK = 64
EPS = 1e-6
NEG = -1e30
LANES = 128
HALF_LANES = LANES // 2

T5_BUCKETS = 32
T5_MAX_DIST = 128

MLA_HEADS = 8
MLA_NOPE = 64
MLA_ROPE = 32
MLA_V = 64
MLA_QK = MLA_NOPE + MLA_ROPE
MLA_Q_RANK = 384
MLA_KV_RANK = 256
ROPE_THETA = 10000.0

SWA_HEADS = 8
SWA_KV_HEADS = 2
SWA_GROUP = SWA_HEADS // SWA_KV_HEADS
SWA_HEAD_DIM = 64
SWA_BAND = 2

CA_HEADS = 16
CA_HEAD_DIM = 64
CA_LEFT_CHUNKS = 8
CA_REL_PAST = 256

PEER_HEADS = 8
PEER_N_KEYS = 128
PEER_HALF = 128
PEER_TOPK = 16

VMEM_LIMIT = 56 * 1024 * 1024

TM_PROJ = 256
TQ_MLA = 256
TQ_BAND = 128
TQ_PEER = 512
EB_PEER = 1024

GELU_C0 = math.sqrt(2.0 / math.pi)
GELU_C1 = GELU_C0 * 0.044715


def _cparams(sem):
    return pltpu.CompilerParams(dimension_semantics=sem, vmem_limit_bytes=VMEM_LIMIT)


def _rms(x, g):
    ms = jnp.mean(x * x, axis=-1, keepdims=True)
    return x * lax.rsqrt(ms + EPS) * g


def _rms_two_heads(x, g2, scale):
    lane = lax.broadcasted_iota(jnp.int32, x.shape, 1)
    lo_mask = lane < HALF_LANES
    s = x * x
    lo = jnp.sum(jnp.where(lo_mask, s, 0.0), axis=-1, keepdims=True)
    hi = jnp.sum(jnp.where(lo_mask, 0.0, s), axis=-1, keepdims=True)
    r = jnp.where(lo_mask, lax.rsqrt(lo * (1.0 / HALF_LANES) + EPS), lax.rsqrt(hi * (1.0 / HALF_LANES) + EPS))
    return x * r * (g2 * scale)


def _dot_nt(a, b):
    return lax.dot_general(a, b, (((1,), (1,)), ((), ())), preferred_element_type=F32)


def _in0_kernel(x_ref, gmix_ref, win_ref, gcq_ref, wq_ref, gq_ref, gckv_ref, wkc_ref, wkp_ref, gk_ref,
                wv_ref, sgq_ref, sgk_ref, rc_ref, rs1_ref, rs2_ref,
                qm_ref, km_ref, vm_ref, qs_ref, ks_ref, vs_ref):
    xn = _rms(x_ref[...], gmix_ref[...]).astype(BF16)
    p = jnp.dot(xn, win_ref[...], preferred_element_type=F32)
    cq = _rms(p[:, 0:MLA_Q_RANK], gcq_ref[...]).astype(BF16)
    ckv = _rms(p[:, 384:640], gckv_ref[...]).astype(BF16)
    kpe = p[:, 1408:1536].astype(BF16)
    q = jnp.dot(cq, wq_ref[...], preferred_element_type=F32)
    k = (jnp.dot(ckv, wkc_ref[...], preferred_element_type=F32)
         + jnp.dot(kpe, wkp_ref[...], preferred_element_type=F32))
    vm_ref[...] = jnp.dot(ckv, wv_ref[...], preferred_element_type=F32).astype(BF16)

    rc, rs1, rs2 = rc_ref[...], rs1_ref[...], rs2_ref[...]

    def head_norm_rope(t, g, scale):
        ms = jnp.sum(t * t, axis=-1, keepdims=True) * (1.0 / MLA_QK)
        tn = t * lax.rsqrt(ms + EPS) * g
        tr = tn * rc + pltpu.roll(tn, LANES - MLA_ROPE // 2, 1) * rs1 + pltpu.roll(tn, MLA_ROPE // 2, 1) * rs2
        return (tr * scale).astype(BF16)

    for h in range(MLA_HEADS):
        sl = slice(h * LANES, (h + 1) * LANES)
        qm_ref[:, sl] = head_norm_rope(q[:, sl], gq_ref[...], MLA_QK ** -0.5)
        km_ref[:, sl] = head_norm_rope(k[:, sl], gk_ref[...], 1.0)

    for m in range(SWA_HEADS // 2):
        sl = slice(640 + m * LANES, 640 + (m + 1) * LANES)
        qs_ref[:, m * LANES:(m + 1) * LANES] = _rms_two_heads(p[:, sl], sgq_ref[...], SWA_HEAD_DIM ** -0.5).astype(BF16)
    ks_ref[...] = _rms_two_heads(p[:, 1152:1280], sgk_ref[...], 1.0).astype(BF16)
    vs_ref[...] = p[:, 1280:1408].astype(BF16)


def _in0_call(x2d, seq, gmix, win, gcq, wq, gq, gckv, wkc, wkp, gk, wv, sgq, sgk, rc, rs1, rs2):
    T = x2d.shape[0]
    tm = TM_PROJ
    nblk_seq = seq // tm
    row = lambda i: (i, 0)
    const = lambda i: (0, 0)
    pos = lambda i: (i % nblk_seq, 0)
    full = lambda a: pl.BlockSpec(a.shape, const)
    out_shapes = (
        jax.ShapeDtypeStruct((T, MLA_HEADS * LANES), BF16),
        jax.ShapeDtypeStruct((T, MLA_HEADS * LANES), BF16),
        jax.ShapeDtypeStruct((T, MLA_HEADS * MLA_V), BF16),
        jax.ShapeDtypeStruct((T, SWA_HEADS * SWA_HEAD_DIM), BF16),
        jax.ShapeDtypeStruct((T, SWA_KV_HEADS * SWA_HEAD_DIM), BF16),
        jax.ShapeDtypeStruct((T, SWA_KV_HEADS * SWA_HEAD_DIM), BF16),
    )
    return pl.pallas_call(
        _in0_kernel,
        grid=(T // tm,),
        in_specs=[pl.BlockSpec((tm, D_MODEL), row), full(gmix), full(win), full(gcq), full(wq), full(gq),
                  full(gckv), full(wkc), full(wkp), full(gk), full(wv), full(sgq), full(sgk),
                  pl.BlockSpec((tm, LANES), pos), pl.BlockSpec((tm, LANES), pos), pl.BlockSpec((tm, LANES), pos)],
        out_specs=tuple(pl.BlockSpec((tm, s.shape[1]), row) for s in out_shapes),
        out_shape=out_shapes,
        compiler_params=_cparams(("parallel",)),
        name="layer0_in_proj",
    )(x2d, gmix, win, gcq, wq, gq, gckv, wkc, wkp, gk, wv, sgq, sgk, rc, rs1, rs2)


def _mla_kernel(q_ref, k_ref, v_ref, o_ref):
    i = pl.program_id(1)
    tq = TQ_MLA
    rows = lax.broadcasted_iota(jnp.int32, (tq, tq), 0)
    cols = lax.broadcasted_iota(jnp.int32, (tq, tq), 1)
    diag_ok = (cols // CHUNK) <= (rows // CHUNK)
    lane = lax.broadcasted_iota(jnp.int32, (tq, LANES), 1)
    for hp in range(MLA_HEADS // 2):
        outs = []
        for hh in range(2):
            h = 2 * hp + hh
            q = q_ref[0, :, h * LANES:(h + 1) * LANES]

            def body(j, carry, q=q, h=h, hp=hp):
                m, l, acc = carry
                start = pl.multiple_of(j * tq, tq)
                kb = k_ref[0, pl.ds(start, tq), h * LANES:(h + 1) * LANES]
                vb = v_ref[0, pl.ds(start, tq), hp * LANES:(hp + 1) * LANES]
                s = _dot_nt(q, kb)
                s = jnp.where(jnp.logical_or(j < i, diag_ok), s, NEG)
                m_new = jnp.maximum(m, jnp.max(s, axis=-1, keepdims=True))
                alpha = jnp.exp(m - m_new)
                p = jnp.exp(s - m_new)
                l = alpha * l + jnp.sum(p, axis=-1, keepdims=True)
                acc = alpha * acc + jnp.dot(p.astype(BF16), vb, preferred_element_type=F32)
                return m_new, l, acc

            init = (jnp.full((tq, 1), NEG, F32), jnp.zeros((tq, 1), F32), jnp.zeros((tq, LANES), F32))
            m, l, acc = lax.fori_loop(0, i + 1, body, init)
            outs.append(acc / l)
        o_ref[0, :, hp * LANES:(hp + 1) * LANES] = jnp.where(lane < HALF_LANES, outs[0], outs[1]).astype(BF16)


def _mla_call(q, k, v):
    B, S, _ = q.shape
    tq = TQ_MLA
    return pl.pallas_call(
        _mla_kernel,
        grid=(B, S // tq),
        in_specs=[pl.BlockSpec((1, tq, q.shape[2]), lambda b, i: (b, i, 0)),
                  pl.BlockSpec((1, S, k.shape[2]), lambda b, i: (b, 0, 0)),
                  pl.BlockSpec((1, S, v.shape[2]), lambda b, i: (b, 0, 0))],
        out_specs=pl.BlockSpec((1, tq, v.shape[2]), lambda b, i: (b, i, 0)),
        out_shape=jax.ShapeDtypeStruct((B, S, v.shape[2]), BF16),
        compiler_params=_cparams(("parallel", "arbitrary")),
        name="latent_attention",
    )(q, k, v)


def _swa_kernel(sink_ref, q_ref, kp_ref, kc_ref, vp_ref, vc_ref, bias_ref, o_ref):
    i = pl.program_id(1)
    tq = TQ_BAND
    kk = jnp.concatenate([kp_ref[0], kc_ref[0]], axis=0)
    vv = jnp.concatenate([vp_ref[0], vc_ref[0]], axis=0)
    lane = lax.broadcasted_iota(jnp.int32, (tq, LANES), 1)
    col = lax.broadcasted_iota(jnp.int32, (tq, 2 * tq), 1)
    key_ok = jnp.logical_or(col >= tq, i > 0)
    for m in range(SWA_HEADS // 2):
        qq = q_ref[0, :, m * LANES:(m + 1) * LANES]
        zero = jnp.zeros_like(qq)
        outs = []
        for hh in range(2):
            h = m + hh * SWA_GROUP
            qx = jnp.where(lane < HALF_LANES, qq, zero) if hh == 0 else jnp.where(lane < HALF_LANES, zero, qq)
            s = _dot_nt(qx, kk) + bias_ref[h]
            s = jnp.where(key_ok, s, NEG)
            sink = sink_ref[h]
            mx = jnp.maximum(jnp.max(s, axis=-1, keepdims=True), sink)
            p = jnp.exp(s - mx)
            den = jnp.sum(p, axis=-1, keepdims=True) + jnp.exp(sink - mx)
            o = jnp.dot(p.astype(BF16), vv, preferred_element_type=F32)
            outs.append(o / den)
        o_ref[0, :, m * LANES:(m + 1) * LANES] = jnp.where(lane < HALF_LANES, outs[0], outs[1]).astype(BF16)


def _swa_call(sinks, q, k, v, bias):
    B, S, _ = q.shape
    tq = TQ_BAND
    prev = lambda b, i: (b, jnp.maximum(i - 1, 0), 0)
    cur = lambda b, i: (b, i, 0)
    return pl.pallas_call(
        _swa_kernel,
        grid=(B, S // tq),
        in_specs=[pl.BlockSpec(memory_space=pltpu.SMEM),
                  pl.BlockSpec((1, tq, q.shape[2]), cur),
                  pl.BlockSpec((1, tq, LANES), prev), pl.BlockSpec((1, tq, LANES), cur),
                  pl.BlockSpec((1, tq, LANES), prev), pl.BlockSpec((1, tq, LANES), cur),
                  pl.BlockSpec(bias.shape, lambda b, i: (0, 0, 0))],
        out_specs=pl.BlockSpec((1, tq, q.shape[2]), cur),
        out_shape=jax.ShapeDtypeStruct(q.shape, BF16),
        compiler_params=_cparams(("parallel", "arbitrary")),
        name="sliding_window_attention",
    )(sinks, q, k, k, v, v, bias)


CA_KBLOCKS = CA_LEFT_CHUNKS * CHUNK // TQ_BAND + 1


def _ca_kernel(q_ref, k_ref, v_ref, bias_ref, o_ref):
    i = pl.program_id(1)
    tq = TQ_BAND
    nk = CA_KBLOCKS * tq
    lane = lax.broadcasted_iota(jnp.int32, (tq, LANES), 1)
    col = lax.broadcasted_iota(jnp.int32, (tq, nk), 1)
    key_ok = (col // tq) >= (CA_KBLOCKS - 1 - i)
    starts = [pl.multiple_of(jnp.maximum(i - (CA_KBLOCKS - 1) + j, 0) * tq, tq) for j in range(CA_KBLOCKS)]
    for m in range(CA_HEADS // 2):
        sl = slice(m * LANES, (m + 1) * LANES)
        kk = jnp.concatenate([k_ref[0, pl.ds(st, tq), sl] for st in starts], axis=0)
        vv = jnp.concatenate([v_ref[0, pl.ds(st, tq), sl] for st in starts], axis=0)
        qq = q_ref[0, :, sl]
        zero = jnp.zeros_like(qq)
        outs = []
        for hh in range(2):
            qx = jnp.where(lane < HALF_LANES, qq, zero) if hh == 0 else jnp.where(lane < HALF_LANES, zero, qq)
            s = _dot_nt(qx, kk) + bias_ref[2 * m + hh]
            s = jnp.where(key_ok, s, NEG)
            mx = jnp.max(s, axis=-1, keepdims=True)
            p = jnp.exp(s - mx)
            den = jnp.sum(p, axis=-1, keepdims=True)
            o = jnp.dot(p.astype(BF16), vv, preferred_element_type=F32)
            outs.append(o / den)
        o_ref[0, :, sl] = jnp.where(lane < HALF_LANES, outs[0], outs[1]).astype(BF16)


def _ca_call(q, k, v, bias):
    B, S, W = q.shape
    tq = TQ_BAND
    return pl.pallas_call(
        _ca_kernel,
        grid=(B, S // tq),
        in_specs=[pl.BlockSpec((1, tq, W), lambda b, i: (b, i, 0)),
                  pl.BlockSpec((1, S, W), lambda b, i: (b, 0, 0)),
                  pl.BlockSpec((1, S, W), lambda b, i: (b, 0, 0)),
                  pl.BlockSpec(bias.shape, lambda b, i: (0, 0, 0))],
        out_specs=pl.BlockSpec((1, tq, W), lambda b, i: (b, i, 0)),
        out_shape=jax.ShapeDtypeStruct(q.shape, BF16),
        compiler_params=_cparams(("parallel", "arbitrary")),
        name="chunk_attention",
    )(q, k, v, bias)


def _in1_kernel(x_ref, gmix_ref, win_ref, gq_ref, gk_ref, q_ref, k_ref, v_ref):
    xn = _rms(x_ref[...], gmix_ref[...]).astype(BF16)
    p = jnp.dot(xn, win_ref[...], preferred_element_type=F32)
    w = CA_HEADS * CA_HEAD_DIM
    for m in range(CA_HEADS // 2):
        sl = slice(m * LANES, (m + 1) * LANES)
        q_ref[:, sl] = _rms_two_heads(p[:, m * LANES:(m + 1) * LANES], gq_ref[...], CA_HEAD_DIM ** -0.5).astype(BF16)
        k_ref[:, sl] = _rms_two_heads(p[:, w + m * LANES:w + (m + 1) * LANES], gk_ref[...], 1.0).astype(BF16)
    v_ref[...] = p[:, 2 * w:3 * w].astype(BF16)


def _in1_call(x2d, gmix, win, gq, gk):
    T = x2d.shape[0]
    tm = TM_PROJ
    w = CA_HEADS * CA_HEAD_DIM
    row = lambda i: (i, 0)
    const = lambda i: (0, 0)
    full = lambda a: pl.BlockSpec(a.shape, const)
    out_shapes = tuple(jax.ShapeDtypeStruct((T, w), BF16) for _ in range(3))
    return pl.pallas_call(
        _in1_kernel,
        grid=(T // tm,),
        in_specs=[pl.BlockSpec((tm, D_MODEL), row), full(gmix), full(win), full(gq), full(gk)],
        out_specs=tuple(pl.BlockSpec((tm, w), row) for _ in range(3)),
        out_shape=out_shapes,
        compiler_params=_cparams(("parallel",)),
        name="layer1_in_proj",
    )(x2d, gmix, win, gq, gk)


def _out_kernel(x_ref, o_ref, w_ref, g_ref, xo_ref, xnt_ref):
    y = x_ref[...] + jnp.dot(o_ref[...], w_ref[...], preferred_element_type=F32)
    xo_ref[...] = y
    xnt_ref[...] = _rms(y, g_ref[...]).T.astype(BF16)


def _out_call(x2d, o2d, w, g):
    T = x2d.shape[0]
    tm = TM_PROJ
    row = lambda i: (i, 0)
    const = lambda i: (0, 0)
    return pl.pallas_call(
        _out_kernel,
        grid=(T // tm,),
        in_specs=[pl.BlockSpec((tm, D_MODEL), row), pl.BlockSpec((tm, o2d.shape[1]), row),
                  pl.BlockSpec(w.shape, const), pl.BlockSpec(g.shape, const)],
        out_specs=(pl.BlockSpec((tm, D_MODEL), row), pl.BlockSpec((D_MODEL, tm), lambda i: (0, i))),
        out_shape=(jax.ShapeDtypeStruct((T, D_MODEL), F32), jax.ShapeDtypeStruct((D_MODEL, T), BF16)),
        compiler_params=_cparams(("parallel",)),
        name="out_proj_residual",
    )(x2d, o2d, w, g)


NTB_PEER = TQ_PEER // LANES
I1_PER_STEP = EB_PEER // PEER_N_KEYS


def _sorted_top16(e):
    rows = lax.broadcasted_iota(jnp.int32, (PEER_TOPK, LANES), 0)
    out = jnp.zeros((PEER_TOPK, LANES), F32)
    cur = e
    for j in range(PEER_TOPK):
        mx = jnp.max(cur, axis=0, keepdims=True)
        out = jnp.where(rows == j, jnp.maximum(mx, 0.0), out)
        cur = jnp.where(cur == mx, -1.0, cur)
    return out


def _candidates(aj, bk):
    rows8 = lax.broadcasted_iota(jnp.int32, (8, LANES), 0)
    pieces = [aj * bk[0:1, :]]
    for k in range(1, PEER_TOPK):
        nj = PEER_TOPK // (k + 1)
        pieces.append(jnp.where(rows8 < nj, aj[0:8, :] * bk[k:k + 1, :], -1.0))
    return jnp.concatenate(pieces, axis=0)


def _peer_kernel(x_ref, xnt_ref, wqt_ref, sk_ref, u_ref, vt_ref, o_ref,
                 q_scr, sc_scr, e1_scr, e2_scr, thr_scr, a_scr, g_scr, acc_scr):
    e = pl.program_id(1)

    @pl.when(e == 0)
    def _route():
        q_scr[...] = jnp.dot(wqt_ref[...], xnt_ref[...], preferred_element_type=F32).astype(BF16)
        acc_scr[...] = jnp.zeros_like(acc_scr)

        def per_head(h, _):
            for p in range(2):
                hp = h * 2 + p
                qs = q_scr[pl.ds(pl.multiple_of(hp * PEER_HALF, PEER_HALF), PEER_HALF), :]
                sc = jnp.dot(sk_ref[hp], qs, preferred_element_type=F32)
                for tb in range(NTB_PEER):
                    sc_scr[p, tb] = sc[:, tb * LANES:(tb + 1) * LANES]

            def per_tile(tb, _):
                s1 = sc_scr[0, tb]
                s2 = sc_scr[1, tb]
                ea = jnp.exp(s1 - jnp.max(s1, axis=0, keepdims=True))
                eb = jnp.exp(s2 - jnp.max(s2, axis=0, keepdims=True))
                aj = _sorted_top16(ea)
                bk = _sorted_top16(eb)
                cand = _candidates(aj, bk)
                cur = cand
                z = jnp.zeros((1, LANES), F32)
                tau = z
                for _j in range(PEER_TOPK):
                    mx = jnp.max(cur, axis=0, keepdims=True)
                    tau = jnp.maximum(mx, 0.0)
                    z = z + tau
                    cur = jnp.where(cur == mx, -1.0, cur)
                rz = 0.5 / z
                cand_n = _candidates(aj * rz, bk)
                thr = jnp.min(jnp.where(cand >= tau, cand_n, 3.0e38), axis=0, keepdims=True)
                e1_scr[h, tb] = ea * rz
                e2_scr[h, tb] = eb
                thr_scr[h, tb] = jnp.broadcast_to(thr, (8, LANES))
                return 0

            lax.fori_loop(0, NTB_PEER, per_tile, 0)
            return 0

        lax.fori_loop(0, PEER_HEADS, per_head, 0)

    a_scr[...] = jnp.dot(u_ref[...], xnt_ref[...], preferred_element_type=F32)

    def per_i1(r, _):
        i1 = e * I1_PER_STEP + r
        r0 = pl.multiple_of(r * PEER_N_KEYS, PEER_N_KEYS)
        for tb in range(NTB_PEER):
            wsum = jnp.zeros((PEER_N_KEYS, LANES), F32)
            for h in range(PEER_HEADS):
                w = e2_scr[h, tb] * e1_scr[h, tb, pl.ds(i1, 1), :]
                wsum = wsum + jnp.where(w >= thr_scr[h, tb, 0:1, :], w, 0.0)
            a = a_scr[pl.ds(r0, PEER_N_KEYS), tb * LANES:(tb + 1) * LANES]
            g = wsum * a * (1.0 + jnp.tanh(a * (GELU_C0 + GELU_C1 * (a * a))))
            g_scr[pl.ds(r0, PEER_N_KEYS), tb * LANES:(tb + 1) * LANES] = g.astype(BF16)
        return 0

    lax.fori_loop(0, I1_PER_STEP, per_i1, 0)
    acc_scr[...] += jnp.dot(vt_ref[...], g_scr[...], preferred_element_type=F32)

    @pl.when(e == pl.num_programs(1) - 1)
    def _finish():
        o_ref[...] = x_ref[...] + acc_scr[...].T


def _peer_call(x2d, xnt, wqt, sk, u, vt):
    T = x2d.shape[0]
    tq, eb = TQ_PEER, EB_PEER
    n_exp = u.shape[0]
    return pl.pallas_call(
        _peer_kernel,
        grid=(T // tq, n_exp // eb),
        in_specs=[pl.BlockSpec((tq, D_MODEL), lambda t, e: (t, 0)),
                  pl.BlockSpec((D_MODEL, tq), lambda t, e: (0, t)),
                  pl.BlockSpec(wqt.shape, lambda t, e: (0, 0)),
                  pl.BlockSpec(sk.shape, lambda t, e: (0, 0, 0)),
                  pl.BlockSpec((eb, D_MODEL), lambda t, e: (e, 0)),
                  pl.BlockSpec((D_MODEL, eb), lambda t, e: (0, e))],
        out_specs=pl.BlockSpec((tq, D_MODEL), lambda t, e: (t, 0)),
        out_shape=jax.ShapeDtypeStruct((T, D_MODEL), F32),
        scratch_shapes=[
            pltpu.VMEM((2 * PEER_HEADS * PEER_HALF, tq), BF16),
            pltpu.VMEM((2, NTB_PEER, PEER_N_KEYS, LANES), F32),
            pltpu.VMEM((PEER_HEADS, NTB_PEER, PEER_N_KEYS, LANES), F32),
            pltpu.VMEM((PEER_HEADS, NTB_PEER, PEER_N_KEYS, LANES), F32),
            pltpu.VMEM((PEER_HEADS, NTB_PEER, 8, LANES), F32),
            pltpu.VMEM((eb, tq), F32),
            pltpu.VMEM((eb, tq), BF16),
            pltpu.VMEM((D_MODEL, tq), F32),
        ],
        compiler_params=_cparams(("parallel", "arbitrary")),
        name="peer_dense",
    )(x2d, xnt, wqt, sk, u, vt)


def _t5_bucket(rel):
    nb = T5_BUCKETS // 2
    max_exact = nb // 2
    ret = jnp.where(rel > 0, nb, 0)
    n = jnp.abs(rel)
    large = max_exact + (jnp.log(jnp.maximum(n, 1).astype(F32) / max_exact)
                         / math.log(T5_MAX_DIST / max_exact) * (nb - max_exact)).astype(jnp.int32)
    large = jnp.minimum(large, nb - 1)
    return ret + jnp.where(n < max_exact, n, large)


def _swa_bias(t5_table):
    tq = TQ_BAND
    r = jnp.arange(tq)[:, None]
    c = jnp.arange(2 * tq)[None, :]
    rel = c - tq - r
    bias = t5_table[_t5_bucket(rel)].astype(F32).transpose(2, 0, 1)
    qc = (tq + r) // CHUNK
    kc = c // CHUNK
    vis = jnp.logical_and(kc >= qc - SWA_BAND, kc <= qc)
    return jnp.where(vis[None], bias, NEG)


def _ca_bias(rel_table):
    tq = TQ_BAND
    left = CA_LEFT_CHUNKS * CHUNK
    r = jnp.arange(tq)[:, None]
    c = jnp.arange(left + tq)[None, :]
    rel = c - left - r
    idx = jnp.clip(rel, -CA_REL_PAST, CHUNK - 1) + CA_REL_PAST
    bias = rel_table[:, idx].astype(F32)
    qc = (left + r) // CHUNK
    kc = c // CHUNK
    vis = jnp.logical_and(kc >= qc - CA_LEFT_CHUNKS, kc <= qc)
    return jnp.where(vis[None], bias, NEG)


def _rope_tables(seq):
    half = MLA_ROPE // 2
    inv = ROPE_THETA ** (-jnp.arange(0, MLA_ROPE, 2, dtype=F32) / MLA_ROPE)
    ang = jnp.arange(seq, dtype=F32)[:, None] * inv[None, :]
    cos, sin = jnp.cos(ang), jnp.sin(ang)
    z = lambda n: jnp.zeros((seq, n), F32)
    tail = LANES - MLA_QK
    rc = jnp.concatenate([jnp.ones((seq, MLA_NOPE), F32), cos, cos, z(tail)], axis=1)
    rs1 = jnp.concatenate([z(MLA_NOPE), -sin, z(half), z(tail)], axis=1)
    rs2 = jnp.concatenate([z(MLA_NOPE), z(half), sin, z(tail)], axis=1)
    return rc, rs1, rs2


SWA_HEAD_ORDER = tuple(h for m in range(SWA_GROUP) for h in (m, m + SWA_GROUP))


def _layer0_weights(ev_w_in, ev_w_out, mla_w_uq, mla_w_ukv, mla_g_q, mla_g_k, swa_g_q, swa_g_k):
    pad_head = LANES - MLA_QK
    mla_in = MLA_Q_RANK + MLA_KV_RANK + MLA_ROPE
    nq = SWA_HEADS * SWA_HEAD_DIM
    nkv = SWA_KV_HEADS * SWA_HEAD_DIM
    order = jnp.asarray(SWA_HEAD_ORDER)
    w_cq_ckv = ev_w_in[:, :MLA_Q_RANK + MLA_KV_RANK]
    w_kpe = ev_w_in[:, MLA_Q_RANK + MLA_KV_RANK:mla_in]
    w_sq = ev_w_in[:, mla_in:mla_in + nq].reshape(D_MODEL, SWA_HEADS, SWA_HEAD_DIM)[:, order].reshape(D_MODEL, nq)
    w_skv = ev_w_in[:, mla_in + nq:mla_in + nq + 2 * nkv]
    win = jnp.concatenate([w_cq_ckv, w_sq, w_skv, w_kpe, jnp.zeros((D_MODEL, LANES - MLA_ROPE), F32)], axis=1)
    wq = jnp.pad(mla_w_uq, ((0, 0), (0, 0), (0, pad_head))).reshape(MLA_Q_RANK, MLA_HEADS * LANES)
    wkc = jnp.pad(mla_w_ukv[:, :, :MLA_NOPE], ((0, 0), (0, 0), (0, LANES - MLA_NOPE))).reshape(MLA_KV_RANK, MLA_HEADS * LANES)
    place = jnp.pad(jnp.eye(MLA_ROPE, dtype=F32), ((0, LANES - MLA_ROPE), (MLA_NOPE, pad_head)))
    wkp = jnp.tile(place, (1, MLA_HEADS))
    wv = mla_w_ukv[:, :, MLA_NOPE:].reshape(MLA_KV_RANK, MLA_HEADS * MLA_V)
    gq = jnp.pad(mla_g_q, (0, pad_head)).reshape(1, LANES)
    gk = jnp.pad(mla_g_k, (0, pad_head)).reshape(1, LANES)
    sgq = jnp.tile(swa_g_q, 2).reshape(1, LANES)
    sgk = jnp.tile(swa_g_k, 2).reshape(1, LANES)
    n_a = MLA_HEADS * MLA_V
    w_out_b = ev_w_out[n_a:].reshape(SWA_HEADS, SWA_HEAD_DIM, D_MODEL)[order].reshape(nq, D_MODEL)
    wout = jnp.concatenate([ev_w_out[:n_a], w_out_b], axis=0)
    return dict(win=win.astype(BF16), wq=wq.astype(BF16), wkc=wkc.astype(BF16), wkp=wkp.astype(BF16),
                wv=wv.astype(BF16), gq=gq, gk=gk, sgq=sgq, sgk=sgk, wout=wout.astype(BF16))


def _peer_layer(x2d, xnt, w_query, sub_keys, u_emb, v_emb):
    wqt = w_query.reshape(D_MODEL, 2 * PEER_HEADS * PEER_HALF).T.astype(BF16)
    sk = sub_keys.reshape(2 * PEER_HEADS, PEER_N_KEYS, PEER_HALF).astype(BF16)
    return _peer_call(x2d, xnt, wqt, sk, u_emb.astype(BF16), v_emb.T.astype(BF16))


def kernel(x, t5_bias, norm_mix, norm_ffn, ev_w_in, ev_w_out, mla_g_cq, mla_w_uq, mla_g_ckv, mla_w_ukv,
           mla_g_q, mla_g_k, swa_g_q, swa_g_k, swa_sinks, od_w_in, od_w_out, ca_g_q, ca_g_k, ca_rel_bias,
           peer_w_query, peer_sub_keys, peer_u, peer_v):
    B, S, D = x.shape
    T = B * S
    x2d = x.reshape(T, D)
    row = lambda a: a.reshape(1, -1)

    w0 = _layer0_weights(ev_w_in[0], ev_w_out[0], mla_w_uq[0], mla_w_ukv[0], mla_g_q[0], mla_g_k[0],
                         swa_g_q[0], swa_g_k[0])
    rc, rs1, rs2 = _rope_tables(S)
    qm, km, vm, qs, ks, vs = _in0_call(
        x2d, S, row(norm_mix[0]), w0["win"], row(mla_g_cq[0]), w0["wq"], w0["gq"], row(mla_g_ckv[0]),
        w0["wkc"], w0["wkp"], w0["gk"], w0["wv"], w0["sgq"], w0["sgk"], rc, rs1, rs2)
    b3 = lambda a: a.reshape(B, S, a.shape[1])
    o_a = _mla_call(b3(qm), b3(km), b3(vm))
    o_b = _swa_call(swa_sinks[0], b3(qs), b3(ks), b3(vs), _swa_bias(t5_bias))
    o0 = jnp.concatenate([o_a, o_b], axis=-1).reshape(T, -1)
    x2d, xnt = _out_call(x2d, o0, w0["wout"], row(norm_ffn[0]))
    x2d = _peer_layer(x2d, xnt, peer_w_query[0], peer_sub_keys[0], peer_u[0], peer_v[0])

    gq2 = jnp.tile(ca_g_q[0], 2).reshape(1, LANES)
    gk2 = jnp.tile(ca_g_k[0], 2).reshape(1, LANES)
    qc, kc, vc = _in1_call(x2d, row(norm_mix[1]), od_w_in[0].astype(BF16), gq2, gk2)
    o_c = _ca_call(b3(qc), b3(kc), b3(vc), _ca_bias(ca_rel_bias[0]))
    x2d, xnt = _out_call(x2d, o_c.reshape(T, -1), od_w_out[0].astype(BF16), row(norm_ffn[1]))
    x2d = _peer_layer(x2d, xnt, peer_w_query[1], peer_sub_keys[1], peer_u[1], peer_v[1])
    return x2d.reshape(B, S, D)
```

```python
import functools
import math

import jax
import jax.numpy as jnp
import numpy as np
from jax import lax
from jax.experimental import pallas as pl
from jax.experimental.pallas import tpu as pltpu

F32 = jnp.float32
BF16 = jnp.bfloat16

D_MODEL = 1024
CHUNK = 64
EPS = 1e-6
NEG = -1e30
LANES = 128
HALF_LANES = LANES // 2

T5_BUCKETS = 32
T5_MAX_DIST = 128

MLA_HEADS = 8
MLA_NOPE = 64
MLA_ROPE = 32
MLA_V = 64
MLA_QK = MLA_NOPE + MLA_ROPE
MLA_Q_RANK = 384
MLA_KV_RANK = 256
ROPE_THETA = 10000.0

SWA_HEADS = 8
SWA_KV_HEADS = 2
SWA_GROUP = SWA_HEADS // SWA_KV_HEADS
SWA_HEAD_DIM = 64
SWA_BAND = 2

CA_HEADS = 16
CA_HEAD_DIM = 64
CA_LEFT_CHUNKS = 8
CA_REL_PAST = 256

PEER_HEADS = 8
PEER_N_KEYS = 128
PEER_HALF = 128
PEER_TOPK = 16

VMEM_LIMIT = 56 * 1024 * 1024

TM_PROJ = 256
TQ_MLA = 256
TQ_BAND = 128
TQ_PEER = 512
EB_PEER = 1024
TS_PEER = 256
CH_PEER = 256

GELU_C0 = math.sqrt(2.0 / math.pi)
GELU_C1 = GELU_C0 * 0.044715


def _cparams(sem):
    return pltpu.CompilerParams(dimension_semantics=sem, vmem_limit_bytes=VMEM_LIMIT)


def _rms(x, g):
    ms = jnp.mean(x * x, axis=-1, keepdims=True)
    return x * lax.rsqrt(ms + EPS) * g


def _rms_two_heads(x, g2, scale):
    lane = lax.broadcasted_iota(jnp.int32, x.shape, 1)
    lo_mask = lane < HALF_LANES
    s = x * x
    lo = jnp.sum(jnp.where(lo_mask, s, 0.0), axis=-1, keepdims=True)
    hi = jnp.sum(jnp.where(lo_mask, 0.0, s), axis=-1, keepdims=True)
    r = jnp.where(lo_mask, lax.rsqrt(lo * (1.0 / HALF_LANES) + EPS), lax.rsqrt(hi * (1.0 / HALF_LANES) + EPS))
    return x * r * (g2 * scale)


def _dot_nt(a, b):
    return lax.dot_general(a, b, (((1,), (1,)), ((), ())), preferred_element_type=F32)


def _in0_kernel(x_ref, gmix_ref, win_ref, gcq_ref, wq_ref, gq_ref, gckv_ref, wkc_ref, wkp_ref, gk_ref,
                wv_ref, sgq_ref, sgk_ref, rc_ref, rs1_ref, rs2_ref,
                qm_ref, km_ref, vm_ref, qs_ref, ks_ref, vs_ref):
    xn = _rms(x_ref[...], gmix_ref[...]).astype(BF16)
    p = jnp.dot(xn, win_ref[...], preferred_element_type=F32)
    cq = _rms(p[:, 0:MLA_Q_RANK], gcq_ref[...]).astype(BF16)
    ckv = _rms(p[:, 384:640], gckv_ref[...]).astype(BF16)
    kpe = p[:, 1408:1536].astype(BF16)
    q = jnp.dot(cq, wq_ref[...], preferred_element_type=F32)
    k = (jnp.dot(ckv, wkc_ref[...], preferred_element_type=F32)
         + jnp.dot(kpe, wkp_ref[...], preferred_element_type=F32))
    vm_ref[...] = jnp.dot(ckv, wv_ref[...], preferred_element_type=F32).astype(BF16)

    rc, rs1, rs2 = rc_ref[...], rs1_ref[...], rs2_ref[...]

    def head_norm_rope(t, g, scale):
        ms = jnp.sum(t * t, axis=-1, keepdims=True) * (1.0 / MLA_QK)
        tn = t * lax.rsqrt(ms + EPS) * g
        tr = tn * rc + pltpu.roll(tn, LANES - MLA_ROPE // 2, 1) * rs1 + pltpu.roll(tn, MLA_ROPE // 2, 1) * rs2
        return (tr * scale).astype(BF16)

    for h in range(MLA_HEADS):
        sl = slice(h * LANES, (h + 1) * LANES)
        qm_ref[:, sl] = head_norm_rope(q[:, sl], gq_ref[...], MLA_QK ** -0.5)
        km_ref[:, sl] = head_norm_rope(k[:, sl], gk_ref[...], 1.0)

    for m in range(SWA_HEADS // 2):
        sl = slice(640 + m * LANES, 640 + (m + 1) * LANES)
        qs_ref[:, m * LANES:(m + 1) * LANES] = _rms_two_heads(p[:, sl], sgq_ref[...], SWA_HEAD_DIM ** -0.5).astype(BF16)
    ks_ref[...] = _rms_two_heads(p[:, 1152:1280], sgk_ref[...], 1.0).astype(BF16)
    vs_ref[...] = p[:, 1280:1408].astype(BF16)


def _in0_call(x2d, seq, gmix, win, gcq, wq, gq, gckv, wkc, wkp, gk, wv, sgq, sgk, rc, rs1, rs2):
    T = x2d.shape[0]
    tm = TM_PROJ
    nblk_seq = seq // tm
    row = lambda i: (i, 0)
    const = lambda i: (0, 0)
    pos = lambda i: (i % nblk_seq, 0)
    full = lambda a: pl.BlockSpec(a.shape, const)
    out_shapes = (
        jax.ShapeDtypeStruct((T, MLA_HEADS * LANES), BF16),
        jax.ShapeDtypeStruct((T, MLA_HEADS * LANES), BF16),
        jax.ShapeDtypeStruct((T, MLA_HEADS * MLA_V), BF16),
        jax.ShapeDtypeStruct((T, SWA_HEADS * SWA_HEAD_DIM), BF16),
        jax.ShapeDtypeStruct((T, SWA_KV_HEADS * SWA_HEAD_DIM), BF16),
        jax.ShapeDtypeStruct((T, SWA_KV_HEADS * SWA_HEAD_DIM), BF16),
    )
    return pl.pallas_call(
        _in0_kernel,
        grid=(T // tm,),
        in_specs=[pl.BlockSpec((tm, D_MODEL), row), full(gmix), full(win), full(gcq), full(wq), full(gq),
                  full(gckv), full(wkc), full(wkp), full(gk), full(wv), full(sgq), full(sgk),
                  pl.BlockSpec((tm, LANES), pos), pl.BlockSpec((tm, LANES), pos), pl.BlockSpec((tm, LANES), pos)],
        out_specs=tuple(pl.BlockSpec((tm, s.shape[1]), row) for s in out_shapes),
        out_shape=out_shapes,
        compiler_params=_cparams(("parallel",)),
        name="layer0_in_proj",
    )(x2d, gmix, win, gcq, wq, gq, gckv, wkc, wkp, gk, wv, sgq, sgk, rc, rs1, rs2)


def _mla_kernel(q_ref, k_ref, v_ref, o_ref):
    i = pl.program_id(1)
    tq = TQ_MLA
    rows = lax.broadcasted_iota(jnp.int32, (tq, tq), 0)
    cols = lax.broadcasted_iota(jnp.int32, (tq, tq), 1)
    diag_ok = (cols // CHUNK) <= (rows // CHUNK)
    lane = lax.broadcasted_iota(jnp.int32, (tq, LANES), 1)
    for hp in range(MLA_HEADS // 2):
        outs = []
        for hh in range(2):
            h = 2 * hp + hh
            q = q_ref[0, :, h * LANES:(h + 1) * LANES]

            def body(j, carry, q=q, h=h, hp=hp):
                m, l, acc = carry
                start = pl.multiple_of(j * tq, tq)
                kb = k_ref[0, pl.ds(start, tq), h * LANES:(h + 1) * LANES]
                vb = v_ref[0, pl.ds(start, tq), hp * LANES:(hp + 1) * LANES]
                s = _dot_nt(q, kb)
                s = jnp.where(jnp.logical_or(j < i, diag_ok), s, NEG)
                m_new = jnp.maximum(m, jnp.max(s, axis=-1, keepdims=True))
                alpha = jnp.exp(m - m_new)
                p = jnp.exp(s - m_new)
                l = alpha * l + jnp.sum(p, axis=-1, keepdims=True)
                acc = alpha * acc + jnp.dot(p.astype(BF16), vb, preferred_element_type=F32)
                return m_new, l, acc

            init = (jnp.full((tq, 1), NEG, F32), jnp.zeros((tq, 1), F32), jnp.zeros((tq, LANES), F32))
            m, l, acc = lax.fori_loop(0, i + 1, body, init)
            outs.append(acc / l)
        o_ref[0, :, hp * LANES:(hp + 1) * LANES] = jnp.where(lane < HALF_LANES, outs[0], outs[1]).astype(BF16)


def _mla_call(q, k, v):
    B, S, _ = q.shape
    tq = TQ_MLA
    return pl.pallas_call(
        _mla_kernel,
        grid=(B, S // tq),
        in_specs=[pl.BlockSpec((1, tq, q.shape[2]), lambda b, i: (b, i, 0)),
                  pl.BlockSpec((1, S, k.shape[2]), lambda b, i: (b, 0, 0)),
                  pl.BlockSpec((1, S, v.shape[2]), lambda b, i: (b, 0, 0))],
        out_specs=pl.BlockSpec((1, tq, v.shape[2]), lambda b, i: (b, i, 0)),
        out_shape=jax.ShapeDtypeStruct((B, S, v.shape[2]), BF16),
        compiler_params=_cparams(("parallel", "arbitrary")),
        name="latent_attention",
    )(q, k, v)


def _swa_kernel(sink_ref, q_ref, kp_ref, kc_ref, vp_ref, vc_ref, bias_ref, o_ref):
    i = pl.program_id(1)
    tq = TQ_BAND
    kk = jnp.concatenate([kp_ref[0], kc_ref[0]], axis=0)
    vv = jnp.concatenate([vp_ref[0], vc_ref[0]], axis=0)
    lane = lax.broadcasted_iota(jnp.int32, (tq, LANES), 1)
    col = lax.broadcasted_iota(jnp.int32, (tq, 2 * tq), 1)
    key_ok = jnp.logical_or(col >= tq, i > 0)
    for m in range(SWA_HEADS // 2):
        qq = q_ref[0, :, m * LANES:(m + 1) * LANES]
        zero = jnp.zeros_like(qq)
        outs = []
        for hh in range(2):
            h = m + hh * SWA_GROUP
            qx = jnp.where(lane < HALF_LANES, qq, zero) if hh == 0 else jnp.where(lane < HALF_LANES, zero, qq)
            s = _dot_nt(qx, kk) + bias_ref[h]
            s = jnp.where(key_ok, s, NEG)
            sink = sink_ref[h]
            mx = jnp.maximum(jnp.max(s, axis=-1, keepdims=True), sink)
            p = jnp.exp(s - mx)
            den = jnp.sum(p, axis=-1, keepdims=True) + jnp.exp(sink - mx)
            o = jnp.dot(p.astype(BF16), vv, preferred_element_type=F32)
            outs.append(o / den)
        o_ref[0, :, m * LANES:(m + 1) * LANES] = jnp.where(lane < HALF_LANES, outs[0], outs[1]).astype(BF16)


def _swa_call(sinks, q, k, v, bias):
    B, S, _ = q.shape
    tq = TQ_BAND
    prev = lambda b, i: (b, jnp.maximum(i - 1, 0), 0)
    cur = lambda b, i: (b, i, 0)
    return pl.pallas_call(
        _swa_kernel,
        grid=(B, S // tq),
        in_specs=[pl.BlockSpec(memory_space=pltpu.SMEM),
                  pl.BlockSpec((1, tq, q.shape[2]), cur),
                  pl.BlockSpec((1, tq, LANES), prev), pl.BlockSpec((1, tq, LANES), cur),
                  pl.BlockSpec((1, tq, LANES), prev), pl.BlockSpec((1, tq, LANES), cur),
                  pl.BlockSpec(bias.shape, lambda b, i: (0, 0, 0))],
        out_specs=pl.BlockSpec((1, tq, q.shape[2]), cur),
        out_shape=jax.ShapeDtypeStruct(q.shape, BF16),
        compiler_params=_cparams(("parallel", "arbitrary")),
        name="sliding_window_attention",
    )(sinks, q, k, k, v, v, bias)


CA_KBLOCKS = CA_LEFT_CHUNKS * CHUNK // TQ_BAND + 1


def _ca_kernel(q_ref, k_ref, v_ref, bias_ref, o_ref):
    i = pl.program_id(1)
    tq = TQ_BAND
    nk = CA_KBLOCKS * tq
    lane = lax.broadcasted_iota(jnp.int32, (tq, LANES), 1)
    col = lax.broadcasted_iota(jnp.int32, (tq, nk), 1)
    key_ok = (col // tq) >= (CA_KBLOCKS - 1 - i)
    starts = [pl.multiple_of(jnp.maximum(i - (CA_KBLOCKS - 1) + j, 0) * tq, tq) for j in range(CA_KBLOCKS)]
    for m in range(CA_HEADS // 2):
        sl = slice(m * LANES, (m + 1) * LANES)
        kk = jnp.concatenate([k_ref[0, pl.ds(st, tq), sl] for st in starts], axis=0)
        vv = jnp.concatenate([v_ref[0, pl.ds(st, tq), sl] for st in starts], axis=0)
        qq = q_ref[0, :, sl]
        zero = jnp.zeros_like(qq)
        outs = []
        for hh in range(2):
            qx = jnp.where(lane < HALF_LANES, qq, zero) if hh == 0 else jnp.where(lane < HALF_LANES, zero, qq)
            s = _dot_nt(qx, kk) + bias_ref[2 * m + hh]
            s = jnp.where(key_ok, s, NEG)
            mx = jnp.max(s, axis=-1, keepdims=True)
            p = jnp.exp(s - mx)
            den = jnp.sum(p, axis=-1, keepdims=True)
            o = jnp.dot(p.astype(BF16), vv, preferred_element_type=F32)
            outs.append(o / den)
        o_ref[0, :, sl] = jnp.where(lane < HALF_LANES, outs[0], outs[1]).astype(BF16)


def _ca_call(q, k, v, bias):
    B, S, W = q.shape
    tq = TQ_BAND
    return pl.pallas_call(
        _ca_kernel,
        grid=(B, S // tq),
        in_specs=[pl.BlockSpec((1, tq, W), lambda b, i: (b, i, 0)),
                  pl.BlockSpec((1, S, W), lambda b, i: (b, 0, 0)),
                  pl.BlockSpec((1, S, W), lambda b, i: (b, 0, 0)),
                  pl.BlockSpec(bias.shape, lambda b, i: (0, 0, 0))],
        out_specs=pl.BlockSpec((1, tq, W), lambda b, i: (b, i, 0)),
        out_shape=jax.ShapeDtypeStruct(q.shape, BF16),
        compiler_params=_cparams(("parallel", "arbitrary")),
        name="chunk_attention",
    )(q, k, v, bias)


def _in1_kernel(x_ref, gmix_ref, win_ref, gq_ref, gk_ref, q_ref, k_ref, v_ref):
    xn = _rms(x_ref[...], gmix_ref[...]).astype(BF16)
    p = jnp.dot(xn, win_ref[...], preferred_element_type=F32)
    w = CA_HEADS * CA_HEAD_DIM
    for m in range(CA_HEADS // 2):
        sl = slice(m * LANES, (m + 1) * LANES)
        q_ref[:, sl] = _rms_two_heads(p[:, m * LANES:(m + 1) * LANES], gq_ref[...], CA_HEAD_DIM ** -0.5).astype(BF16)
        k_ref[:, sl] = _rms_two_heads(p[:, w + m * LANES:w + (m + 1) * LANES], gk_ref[...], 1.0).astype(BF16)
    v_ref[...] = p[:, 2 * w:3 * w].astype(BF16)


def _in1_call(x2d, gmix, win, gq, gk):
    T = x2d.shape[0]
    tm = TM_PROJ
    w = CA_HEADS * CA_HEAD_DIM
    row = lambda i: (i, 0)
    const = lambda i: (0, 0)
    full = lambda a: pl.BlockSpec(a.shape, const)
    out_shapes = tuple(jax.ShapeDtypeStruct((T, w), BF16) for _ in range(3))
    return pl.pallas_call(
        _in1_kernel,
        grid=(T // tm,),
        in_specs=[pl.BlockSpec((tm, D_MODEL), row), full(gmix), full(win), full(gq), full(gk)],
        out_specs=tuple(pl.BlockSpec((tm, w), row) for _ in range(3)),
        out_shape=out_shapes,
        compiler_params=_cparams(("parallel",)),
        name="layer1_in_proj",
    )(x2d, gmix, win, gq, gk)


def _out_kernel(x_ref, o_ref, w_ref, g_ref, xo_ref, xnt_ref):
    y = x_ref[...] + jnp.dot(o_ref[...], w_ref[...], preferred_element_type=F32)
    xo_ref[...] = y
    xnt_ref[...] = _rms(y, g_ref[...]).T.astype(BF16)


def _out_call(x2d, o2d, w, g):
    T = x2d.shape[0]
    tm = TM_PROJ
    row = lambda i: (i, 0)
    const = lambda i: (0, 0)
    return pl.pallas_call(
        _out_kernel,
        grid=(T // tm,),
        in_specs=[pl.BlockSpec((tm, D_MODEL), row), pl.BlockSpec((tm, o2d.shape[1]), row),
                  pl.BlockSpec(w.shape, const), pl.BlockSpec(g.shape, const)],
        out_specs=(pl.BlockSpec((tm, D_MODEL), row), pl.BlockSpec((D_MODEL, tm), lambda i: (0, i))),
        out_shape=(jax.ShapeDtypeStruct((T, D_MODEL), F32), jax.ShapeDtypeStruct((D_MODEL, T), BF16)),
        compiler_params=_cparams(("parallel",)),
        name="out_proj_residual",
    )(x2d, o2d, w, g)


NTB_PEER = TQ_PEER // LANES
I1_PER_STEP = EB_PEER // PEER_N_KEYS


def _sorted_top16(e):
    rows = lax.broadcasted_iota(jnp.int32, (PEER_TOPK, LANES), 0)
    out = jnp.zeros((PEER_TOPK, LANES), F32)
    cur = e
    for j in range(PEER_TOPK):
        mx = jnp.max(cur, axis=0, keepdims=True)
        out = jnp.where(rows == j, jnp.maximum(mx, 0.0), out)
        cur = jnp.where(cur == mx, -1.0, cur)
    return out


def _candidates(aj, bk):
    rows8 = lax.broadcasted_iota(jnp.int32, (8, LANES), 0)
    pieces = [aj * bk[0:1, :]]
    for k in range(1, PEER_TOPK):
        nj = PEER_TOPK // (k + 1)
        pieces.append(jnp.where(rows8 < nj, aj[0:8, :] * bk[k:k + 1, :], -1.0))
    return jnp.concatenate(pieces, axis=0)


def _peer_kernel(x_ref, xnt_ref, wqt_ref, sk_ref, u_ref, vt_ref, o_ref,
                 q_scr, sc_scr, e1_scr, e2_scr, thr_scr, g_scr, acc_scr):
    e = pl.program_id(1)

    @pl.when(e == 0)
    def _route():
        q_scr[...] = jnp.dot(wqt_ref[...], xnt_ref[...], preferred_element_type=F32).astype(BF16)
        acc_scr[...] = jnp.zeros_like(acc_scr)

        def per_head(h, _):
            for p in range(2):
                hp = h * 2 + p
                qs = q_scr[pl.ds(pl.multiple_of(hp * PEER_HALF, PEER_HALF), PEER_HALF), :]
                sc = jnp.dot(sk_ref[hp], qs, preferred_element_type=F32)
                for tb in range(NTB_PEER):
                    sc_scr[p, tb] = sc[:, tb * LANES:(tb + 1) * LANES]

            def per_tile(tb, _):
                s1 = sc_scr[0, tb]
                s2 = sc_scr[1, tb]
                ea = jnp.exp(s1 - jnp.max(s1, axis=0, keepdims=True))
                eb = jnp.exp(s2 - jnp.max(s2, axis=0, keepdims=True))
                aj = _sorted_top16(ea)
                bk = _sorted_top16(eb)
                cand = _candidates(aj, bk)
                cur = cand
                z = jnp.zeros((1, LANES), F32)
                tau = z
                for _j in range(PEER_TOPK):
                    mx = jnp.max(cur, axis=0, keepdims=True)
                    tau = jnp.maximum(mx, 0.0)
                    z = z + tau
                    cur = jnp.where(cur == mx, -1.0, cur)
                rz = 0.5 / z
                cand_n = _candidates(aj * rz, bk)
                thr = jnp.min(jnp.where(cand >= tau, cand_n, 3.0e38), axis=0, keepdims=True)
                e1_scr[h, tb] = ea * rz
                e2_scr[h, tb] = eb
                thr_scr[h, tb] = jnp.broadcast_to(thr, (8, LANES))
                return 0

            lax.fori_loop(0, NTB_PEER, per_tile, 0)
            return 0

        lax.fori_loop(0, PEER_HEADS, per_head, 0)

    n_sub = TQ_PEER // TS_PEER
    n_ch = EB_PEER // CH_PEER
    d_slab = D_MODEL // n_ch

    def pre_act(s, c):
        return jnp.dot(u_ref[c * CH_PEER:(c + 1) * CH_PEER, :], xnt_ref[:, s * TS_PEER:(s + 1) * TS_PEER],
                       preferred_element_type=F32)

    def route_weights(s, c, a):
        for r2 in range(CH_PEER // PEER_N_KEYS):
            r = c * (CH_PEER // PEER_N_KEYS) + r2
            i1 = e * I1_PER_STEP + r
            for tb2 in range(TS_PEER // LANES):
                tb = s * (TS_PEER // LANES) + tb2
                wsum = jnp.zeros((PEER_N_KEYS, LANES), F32)
                for h in range(PEER_HEADS):
                    w = e2_scr[h, tb] * e1_scr[h, tb, pl.ds(i1, 1), :]
                    wsum = wsum + jnp.where(w >= thr_scr[h, tb, 0:1, :], w, 0.0)
                at = a[r2 * PEER_N_KEYS:(r2 + 1) * PEER_N_KEYS, tb2 * LANES:(tb2 + 1) * LANES]
                g = wsum * at * (1.0 + jnp.tanh(at * (GELU_C0 + GELU_C1 * (at * at))))
                g_scr[s % 2, r * PEER_N_KEYS:(r + 1) * PEER_N_KEYS, tb2 * LANES:(tb2 + 1) * LANES] = g.astype(BF16)

    def value_slab(s, d):
        ds_ = slice(d * d_slab, (d + 1) * d_slab)
        cs = slice(s * TS_PEER, (s + 1) * TS_PEER)
        acc_scr[ds_, cs] += jnp.dot(vt_ref[ds_, :], g_scr[s % 2], preferred_element_type=F32)

    stages = [(s, c) for s in range(n_sub) for c in range(n_ch)]
    a_next = pre_act(*stages[0])
    for idx, (s, c) in enumerate(stages):
        a_cur = a_next
        if idx + 1 < len(stages):
            a_next = pre_act(*stages[idx + 1])
        route_weights(s, c, a_cur)
        if s > 0:
            value_slab(s - 1, c)
    for d in range(n_ch):
        value_slab(n_sub - 1, d)

    @pl.when(e == pl.num_programs(1) - 1)
    def _finish():
        o_ref[...] = x_ref[...] + acc_scr[...].T


def _peer_call(x2d, xnt, wqt, sk, u, vt):
    T = x2d.shape[0]
    tq, eb = TQ_PEER, EB_PEER
    n_exp = u.shape[0]
    return pl.pallas_call(
        _peer_kernel,
        grid=(T // tq, n_exp // eb),
        in_specs=[pl.BlockSpec((tq, D_MODEL), lambda t, e: (t, 0)),
                  pl.BlockSpec((D_MODEL, tq), lambda t, e: (0, t)),
                  pl.BlockSpec(wqt.shape, lambda t, e: (0, 0)),
                  pl.BlockSpec(sk.shape, lambda t, e: (0, 0, 0)),
                  pl.BlockSpec((eb, D_MODEL), lambda t, e: (e, 0)),
                  pl.BlockSpec((D_MODEL, eb), lambda t, e: (0, e))],
        out_specs=pl.BlockSpec((tq, D_MODEL), lambda t, e: (t, 0)),
        out_shape=jax.ShapeDtypeStruct((T, D_MODEL), F32),
        scratch_shapes=[
            pltpu.VMEM((2 * PEER_HEADS * PEER_HALF, tq), BF16),
            pltpu.VMEM((2, NTB_PEER, PEER_N_KEYS, LANES), F32),
            pltpu.VMEM((PEER_HEADS, NTB_PEER, PEER_N_KEYS, LANES), F32),
            pltpu.VMEM((PEER_HEADS, NTB_PEER, PEER_N_KEYS, LANES), F32),
            pltpu.VMEM((PEER_HEADS, NTB_PEER, 8, LANES), F32),
            pltpu.VMEM((2, eb, TS_PEER), BF16),
            pltpu.VMEM((D_MODEL, tq), F32),
        ],
        compiler_params=_cparams(("parallel", "arbitrary")),
        name="peer_dense",
    )(x2d, xnt, wqt, sk, u, vt)


def _t5_bucket(rel):
    nb = T5_BUCKETS // 2
    max_exact = nb // 2
    ret = jnp.where(rel > 0, nb, 0)
    n = jnp.abs(rel)
    large = max_exact + (jnp.log(jnp.maximum(n, 1).astype(F32) / max_exact)
                         / math.log(T5_MAX_DIST / max_exact) * (nb - max_exact)).astype(jnp.int32)
    large = jnp.minimum(large, nb - 1)
    return ret + jnp.where(n < max_exact, n, large)


def _swa_bias(t5_table):
    tq = TQ_BAND
    r = jnp.arange(tq)[:, None]
    c = jnp.arange(2 * tq)[None, :]
    rel = c - tq - r
    bias = t5_table[_t5_bucket(rel)].astype(F32).transpose(2, 0, 1)
    qc = (tq + r) // CHUNK
    kc = c // CHUNK
    vis = jnp.logical_and(kc >= qc - SWA_BAND, kc <= qc)
    return jnp.where(vis[None], bias, NEG)


def _ca_bias(rel_table):
    tq = TQ_BAND
    left = CA_LEFT_CHUNKS * CHUNK
    r = jnp.arange(tq)[:, None]
    c = jnp.arange(left + tq)[None, :]
    rel = c - left - r
    idx = jnp.clip(rel, -CA_REL_PAST, CHUNK - 1) + CA_REL_PAST
    bias = rel_table[:, idx].astype(F32)
    qc = (left + r) // CHUNK
    kc = c // CHUNK
    vis = jnp.logical_and(kc >= qc - CA_LEFT_CHUNKS, kc <= qc)
    return jnp.where(vis[None], bias, NEG)


def _rope_tables(seq):
    half = MLA_ROPE // 2
    inv = ROPE_THETA ** (-jnp.arange(0, MLA_ROPE, 2, dtype=F32) / MLA_ROPE)
    ang = jnp.arange(seq, dtype=F32)[:, None] * inv[None, :]
    cos, sin = jnp.cos(ang), jnp.sin(ang)
    z = lambda n: jnp.zeros((seq, n), F32)
    tail = LANES - MLA_QK
    rc = jnp.concatenate([jnp.ones((seq, MLA_NOPE), F32), cos, cos, z(tail)], axis=1)
    rs1 = jnp.concatenate([z(MLA_NOPE), -sin, z(half), z(tail)], axis=1)
    rs2 = jnp.concatenate([z(MLA_NOPE), z(half), sin, z(tail)], axis=1)
    return rc, rs1, rs2


SWA_HEAD_ORDER = tuple(h for m in range(SWA_GROUP) for h in (m, m + SWA_GROUP))


def _layer0_weights(ev_w_in, ev_w_out, mla_w_uq, mla_w_ukv, mla_g_q, mla_g_k, swa_g_q, swa_g_k):
    pad_head = LANES - MLA_QK
    mla_in = MLA_Q_RANK + MLA_KV_RANK + MLA_ROPE
    nq = SWA_HEADS * SWA_HEAD_DIM
    nkv = SWA_KV_HEADS * SWA_HEAD_DIM
    order = jnp.asarray(SWA_HEAD_ORDER)
    w_cq_ckv = ev_w_in[:, :MLA_Q_RANK + MLA_KV_RANK]
    w_kpe = ev_w_in[:, MLA_Q_RANK + MLA_KV_RANK:mla_in]
    w_sq = ev_w_in[:, mla_in:mla_in + nq].reshape(D_MODEL, SWA_HEADS, SWA_HEAD_DIM)[:, order].reshape(D_MODEL, nq)
    w_skv = ev_w_in[:, mla_in + nq:mla_in + nq + 2 * nkv]
    win = jnp.concatenate([w_cq_ckv, w_sq, w_skv, w_kpe, jnp.zeros((D_MODEL, LANES - MLA_ROPE), F32)], axis=1)
    wq = jnp.pad(mla_w_uq, ((0, 0), (0, 0), (0, pad_head))).reshape(MLA_Q_RANK, MLA_HEADS * LANES)
    wkc = jnp.pad(mla_w_ukv[:, :, :MLA_NOPE], ((0, 0), (0, 0), (0, LANES - MLA_NOPE))).reshape(MLA_KV_RANK, MLA_HEADS * LANES)
    place = jnp.pad(jnp.eye(MLA_ROPE, dtype=F32), ((0, LANES - MLA_ROPE), (MLA_NOPE, pad_head)))
    wkp = jnp.tile(place, (1, MLA_HEADS))
    wv = mla_w_ukv[:, :, MLA_NOPE:].reshape(MLA_KV_RANK, MLA_HEADS * MLA_V)
    gq = jnp.pad(mla_g_q, (0, pad_head)).reshape(1, LANES)
    gk = jnp.pad(mla_g_k, (0, pad_head)).reshape(1, LANES)
    sgq = jnp.tile(swa_g_q, 2).reshape(1, LANES)
    sgk = jnp.tile(swa_g_k, 2).reshape(1, LANES)
    n_a = MLA_HEADS * MLA_V
    w_out_b = ev_w_out[n_a:].reshape(SWA_HEADS, SWA_HEAD_DIM, D_MODEL)[order].reshape(nq, D_MODEL)
    wout = jnp.concatenate([ev_w_out[:n_a], w_out_b], axis=0)
    return dict(win=win.astype(BF16), wq=wq.astype(BF16), wkc=wkc.astype(BF16), wkp=wkp.astype(BF16),
                wv=wv.astype(BF16), gq=gq, gk=gk, sgq=sgq, sgk=sgk, wout=wout.astype(BF16))


def _peer_layer(x2d, xnt, w_query, sub_keys, u_emb, v_emb):
    wqt = w_query.reshape(D_MODEL, 2 * PEER_HEADS * PEER_HALF).T.astype(BF16)
    sk = sub_keys.reshape(2 * PEER_HEADS, PEER_N_KEYS, PEER_HALF).astype(BF16)
    return _peer_call(x2d, xnt, wqt, sk, u_emb.astype(BF16), v_emb.T.astype(BF16))


def kernel(x, t5_bias, norm_mix, norm_ffn, ev_w_in, ev_w_out, mla_g_cq, mla_w_uq, mla_g_ckv, mla_w_ukv,
           mla_g_q, mla_g_k, swa_g_q, swa_g_k, swa_sinks, od_w_in, od_w_out, ca_g_q, ca_g_k, ca_rel_bias,
           peer_w_query, peer_sub_keys, peer_u, peer_v):
    B, S, D = x.shape
    T = B * S
    x2d = x.reshape(T, D)
    row = lambda a: a.reshape(1, -1)

    w0 = _layer0_weights(ev_w_in[0], ev_w_out[0], mla_w_uq[0], mla_w_ukv[0], mla_g_q[0], mla_g_k[0],
                         swa_g_q[0], swa_g_k[0])
    rc, rs1, rs2 = _rope_tables(S)
    qm, km, vm, qs, ks, vs = _in0_call(
        x2d, S, row(norm_mix[0]), w0["win"], row(mla_g_cq[0]), w0["wq"], w0["gq"], row(mla_g_ckv[0]),
        w0["wkc"], w0["wkp"], w0["gk"], w0["wv"], w0["sgq"], w0["sgk"], rc, rs1, rs2)
    b3 = lambda a: a.reshape(B, S, a.shape[1])
    o_a = _mla_call(b3(qm), b3(km), b3(vm))
    o_b = _swa_call(swa_sinks[0], b3(qs), b3(ks), b3(vs), _swa_bias(t5_bias))
    o0 = jnp.concatenate([o_a, o_b], axis=-1).reshape(T, -1)
    x2d, xnt = _out_call(x2d, o0, w0["wout"], row(norm_ffn[0]))
    x2d = _peer_layer(x2d, xnt, peer_w_query[0], peer_sub_keys[0], peer_u[0], peer_v[0])

    gq2 = jnp.tile(ca_g_q[0], 2).reshape(1, LANES)
    gk2 = jnp.tile(ca_g_k[0], 2).reshape(1, LANES)
    qc, kc, vc = _in1_call(x2d, row(norm_mix[1]), od_w_in[0].astype(BF16), gq2, gk2)
    o_c = _ca_call(b3(qc), b3(kc), b3(vc), _ca_bias(ca_rel_bias[0]))
    x2d, xnt = _out_call(x2d, o_c.reshape(T, -1), od_w_out[0].astype(BF16), row(norm_ffn[1]))
    x2d = _peer_layer(x2d, xnt, peer_w_query[1], peer_sub_keys[1], peer_u[1], peer_v[1])
    return x2d.reshape(B, S, D)
```

```python
import functools
import math

import jax
import jax.numpy as jnp
import numpy as np
from jax import lax
from jax.experimental import pallas as pl
from jax.experimental.pallas import tpu as pltpu

F32 = jnp.float32
BF16 = jnp.bfloat16

D_MODEL = 1024
CHUNK = 64
EPS = 1e-6
NEG = -1e30
LANES = 128
HALF_LANES = LANES // 2

T5_BUCKETS = 32
T5_MAX_DIST = 128

MLA_HEADS = 8
MLA_NOPE = 64
MLA_ROPE = 32
MLA_V = 64
MLA_QK = MLA_NOPE + MLA_ROPE
MLA_Q_RANK = 384
MLA_KV_RANK = 256
ROPE_THETA = 10000.0

SWA_HEADS = 8
SWA_KV_HEADS = 2
SWA_GROUP = SWA_HEADS // SWA_KV_HEADS
SWA_HEAD_DIM = 64
SWA_BAND = 2

CA_HEADS = 16
CA_HEAD_DIM = 64
CA_LEFT_CHUNKS = 8
CA_REL_PAST = 256

PEER_HEADS = 8
PEER_N_KEYS = 128
PEER_HALF = 128
PEER_TOPK = 16

VMEM_LIMIT = 56 * 1024 * 1024

TM_PROJ = 256
TQ_MLA = 256
TQ_BAND = 128
TQ_PEER = 512
EB_PEER = 1024
TS_PEER = 256
CH_PEER = 256

GELU_C0 = math.sqrt(2.0 / math.pi)
GELU_C1 = GELU_C0 * 0.044715


def _cparams(sem):
    return pltpu.CompilerParams(dimension_semantics=sem, vmem_limit_bytes=VMEM_LIMIT)


def _rms(x, g):
    ms = jnp.mean(x * x, axis=-1, keepdims=True)
    return x * lax.rsqrt(ms + EPS) * g


def _rms_two_heads(x, g2, scale):
    lane = lax.broadcasted_iota(jnp.int32, x.shape, 1)
    lo_mask = lane < HALF_LANES
    s = x * x
    lo = jnp.sum(jnp.where(lo_mask, s, 0.0), axis=-1, keepdims=True)
    hi = jnp.sum(jnp.where(lo_mask, 0.0, s), axis=-1, keepdims=True)
    r = jnp.where(lo_mask, lax.rsqrt(lo * (1.0 / HALF_LANES) + EPS), lax.rsqrt(hi * (1.0 / HALF_LANES) + EPS))
    return x * r * (g2 * scale)


def _dot_nt(a, b):
    return lax.dot_general(a, b, (((1,), (1,)), ((), ())), preferred_element_type=F32)


def _in0_kernel(x_ref, gmix_ref, win_ref, gcq_ref, wq_ref, gq_ref, gckv_ref, wkc_ref, wkp_ref, gk_ref,
                wv_ref, sgq_ref, sgk_ref, rc_ref, rs1_ref, rs2_ref,
                qm_ref, km_ref, vm_ref, qs_ref, ks_ref, vs_ref):
    xn = _rms(x_ref[...], gmix_ref[...]).astype(BF16)
    p = jnp.dot(xn, win_ref[...], preferred_element_type=F32)
    cq = _rms(p[:, 0:MLA_Q_RANK], gcq_ref[...]).astype(BF16)
    ckv = _rms(p[:, 384:640], gckv_ref[...]).astype(BF16)
    kpe = p[:, 1408:1536].astype(BF16)
    q = jnp.dot(cq, wq_ref[...], preferred_element_type=F32)
    k = (jnp.dot(ckv, wkc_ref[...], preferred_element_type=F32)
         + jnp.dot(kpe, wkp_ref[...], preferred_element_type=F32))
    vm_ref[...] = jnp.dot(ckv, wv_ref[...], preferred_element_type=F32).astype(BF16)

    rc, rs1, rs2 = rc_ref[...], rs1_ref[...], rs2_ref[...]

    def head_norm_rope(t, g, scale):
        ms = jnp.sum(t * t, axis=-1, keepdims=True) * (1.0 / MLA_QK)
        tn = t * lax.rsqrt(ms + EPS) * g
        tr = tn * rc + pltpu.roll(tn, LANES - MLA_ROPE // 2, 1) * rs1 + pltpu.roll(tn, MLA_ROPE // 2, 1) * rs2
        return (tr * scale).astype(BF16)

    for h in range(MLA_HEADS):
        sl = slice(h * LANES, (h + 1) * LANES)
        qm_ref[:, sl] = head_norm_rope(q[:, sl], gq_ref[...], MLA_QK ** -0.5)
        km_ref[:, sl] = head_norm_rope(k[:, sl], gk_ref[...], 1.0)

    for m in range(SWA_HEADS // 2):
        sl = slice(640 + m * LANES, 640 + (m + 1) * LANES)
        qs_ref[:, m * LANES:(m + 1) * LANES] = _rms_two_heads(p[:, sl], sgq_ref[...], SWA_HEAD_DIM ** -0.5).astype(BF16)
    ks_ref[...] = _rms_two_heads(p[:, 1152:1280], sgk_ref[...], 1.0).astype(BF16)
    vs_ref[...] = p[:, 1280:1408].astype(BF16)


def _in0_call(x2d, seq, gmix, win, gcq, wq, gq, gckv, wkc, wkp, gk, wv, sgq, sgk, rc, rs1, rs2):
    T = x2d.shape[0]
    tm = TM_PROJ
    nblk_seq = seq // tm
    row = lambda i: (i, 0)
    const = lambda i: (0, 0)
    pos = lambda i: (i % nblk_seq, 0)
    full = lambda a: pl.BlockSpec(a.shape, const)
    out_shapes = (
        jax.ShapeDtypeStruct((T, MLA_HEADS * LANES), BF16),
        jax.ShapeDtypeStruct((T, MLA_HEADS * LANES), BF16),
        jax.ShapeDtypeStruct((T, MLA_HEADS * MLA_V), BF16),
        jax.ShapeDtypeStruct((T, SWA_HEADS * SWA_HEAD_DIM), BF16),
        jax.ShapeDtypeStruct((T, SWA_KV_HEADS * SWA_HEAD_DIM), BF16),
        jax.ShapeDtypeStruct((T, SWA_KV_HEADS * SWA_HEAD_DIM), BF16),
    )
    return pl.pallas_call(
        _in0_kernel,
        grid=(T // tm,),
        in_specs=[pl.BlockSpec((tm, D_MODEL), row), full(gmix), full(win), full(gcq), full(wq), full(gq),
                  full(gckv), full(wkc), full(wkp), full(gk), full(wv), full(sgq), full(sgk),
                  pl.BlockSpec((tm, LANES), pos), pl.BlockSpec((tm, LANES), pos), pl.BlockSpec((tm, LANES), pos)],
        out_specs=tuple(pl.BlockSpec((tm, s.shape[1]), row) for s in out_shapes),
        out_shape=out_shapes,
        compiler_params=_cparams(("parallel",)),
        name="layer0_in_proj",
    )(x2d, gmix, win, gcq, wq, gq, gckv, wkc, wkp, gk, wv, sgq, sgk, rc, rs1, rs2)


def _mla_kernel(q_ref, k_ref, v_ref, o_ref):
    i = pl.program_id(1)
    tq = TQ_MLA
    rows = lax.broadcasted_iota(jnp.int32, (tq, tq), 0)
    cols = lax.broadcasted_iota(jnp.int32, (tq, tq), 1)
    diag_ok = (cols // CHUNK) <= (rows // CHUNK)
    lane = lax.broadcasted_iota(jnp.int32, (tq, LANES), 1)
    for hp in range(MLA_HEADS // 2):
        outs = []
        for hh in range(2):
            h = 2 * hp + hh
            q = q_ref[0, :, h * LANES:(h + 1) * LANES]

            def body(j, carry, q=q, h=h, hp=hp):
                m, l, acc = carry
                start = pl.multiple_of(j * tq, tq)
                kb = k_ref[0, pl.ds(start, tq), h * LANES:(h + 1) * LANES]
                vb = v_ref[0, pl.ds(start, tq), hp * LANES:(hp + 1) * LANES]
                s = _dot_nt(q, kb)
                s = jnp.where(jnp.logical_or(j < i, diag_ok), s, NEG)
                m_new = jnp.maximum(m, jnp.max(s, axis=-1, keepdims=True))
                alpha = jnp.exp(m - m_new)
                p = jnp.exp(s - m_new)
                l = alpha * l + jnp.sum(p, axis=-1, keepdims=True)
                acc = alpha * acc + jnp.dot(p.astype(BF16), vb, preferred_element_type=F32)
                return m_new, l, acc

            init = (jnp.full((tq, 1), NEG, F32), jnp.zeros((tq, 1), F32), jnp.zeros((tq, LANES), F32))
            m, l, acc = lax.fori_loop(0, i + 1, body, init)
            outs.append(acc / l)
        o_ref[0, :, hp * LANES:(hp + 1) * LANES] = jnp.where(lane < HALF_LANES, outs[0], outs[1]).astype(BF16)


def _mla_call(q, k, v):
    B, S, _ = q.shape
    tq = TQ_MLA
    return pl.pallas_call(
        _mla_kernel,
        grid=(B, S // tq),
        in_specs=[pl.BlockSpec((1, tq, q.shape[2]), lambda b, i: (b, i, 0)),
                  pl.BlockSpec((1, S, k.shape[2]), lambda b, i: (b, 0, 0)),
                  pl.BlockSpec((1, S, v.shape[2]), lambda b, i: (b, 0, 0))],
        out_specs=pl.BlockSpec((1, tq, v.shape[2]), lambda b, i: (b, i, 0)),
        out_shape=jax.ShapeDtypeStruct((B, S, v.shape[2]), BF16),
        compiler_params=_cparams(("parallel", "arbitrary")),
        name="latent_attention",
    )(q, k, v)


def _swa_kernel(sink_ref, q_ref, kp_ref, kc_ref, vp_ref, vc_ref, bias_ref, o_ref):
    i = pl.program_id(1)
    tq = TQ_BAND
    kk = jnp.concatenate([kp_ref[0], kc_ref[0]], axis=0)
    vv = jnp.concatenate([vp_ref[0], vc_ref[0]], axis=0)
    lane = lax.broadcasted_iota(jnp.int32, (tq, LANES), 1)
    col = lax.broadcasted_iota(jnp.int32, (tq, 2 * tq), 1)
    key_ok = jnp.logical_or(col >= tq, i > 0)
    for m in range(SWA_HEADS // 2):
        qq = q_ref[0, :, m * LANES:(m + 1) * LANES]
        zero = jnp.zeros_like(qq)
        outs = []
        for hh in range(2):
            h = m + hh * SWA_GROUP
            qx = jnp.where(lane < HALF_LANES, qq, zero) if hh == 0 else jnp.where(lane < HALF_LANES, zero, qq)
            s = _dot_nt(qx, kk) + bias_ref[h]
            s = jnp.where(key_ok, s, NEG)
            sink = sink_ref[h]
            mx = jnp.maximum(jnp.max(s, axis=-1, keepdims=True), sink)
            p = jnp.exp(s - mx)
            den = jnp.sum(p, axis=-1, keepdims=True) + jnp.exp(sink - mx)
            o = jnp.dot(p.astype(BF16), vv, preferred_element_type=F32)
            outs.append(o / den)
        o_ref[0, :, m * LANES:(m + 1) * LANES] = jnp.where(lane < HALF_LANES, outs[0], outs[1]).astype(BF16)


def _swa_call(sinks, q, k, v, bias):
    B, S, _ = q.shape
    tq = TQ_BAND
    prev = lambda b, i: (b, jnp.maximum(i - 1, 0), 0)
    cur = lambda b, i: (b, i, 0)
    return pl.pallas_call(
        _swa_kernel,
        grid=(B, S // tq),
        in_specs=[pl.BlockSpec(memory_space=pltpu.SMEM),
                  pl.BlockSpec((1, tq, q.shape[2]), cur),
                  pl.BlockSpec((1, tq, LANES), prev), pl.BlockSpec((1, tq, LANES), cur),
                  pl.BlockSpec((1, tq, LANES), prev), pl.BlockSpec((1, tq, LANES), cur),
                  pl.BlockSpec(bias.shape, lambda b, i: (0, 0, 0))],
        out_specs=pl.BlockSpec((1, tq, q.shape[2]), cur),
        out_shape=jax.ShapeDtypeStruct(q.shape, BF16),
        compiler_params=_cparams(("parallel", "arbitrary")),
        name="sliding_window_attention",
    )(sinks, q, k, k, v, v, bias)


CA_KBLOCKS = CA_LEFT_CHUNKS * CHUNK // TQ_BAND + 1


def _ca_kernel(q_ref, k_ref, v_ref, bias_ref, o_ref):
    i = pl.program_id(1)
    tq = TQ_BAND
    nk = CA_KBLOCKS * tq
    lane = lax.broadcasted_iota(jnp.int32, (tq, LANES), 1)
    col = lax.broadcasted_iota(jnp.int32, (tq, nk), 1)
    key_ok = (col // tq) >= (CA_KBLOCKS - 1 - i)
    starts = [pl.multiple_of(jnp.maximum(i - (CA_KBLOCKS - 1) + j, 0) * tq, tq) for j in range(CA_KBLOCKS)]
    for m in range(CA_HEADS // 2):
        sl = slice(m * LANES, (m + 1) * LANES)
        kk = jnp.concatenate([k_ref[0, pl.ds(st, tq), sl] for st in starts], axis=0)
        vv = jnp.concatenate([v_ref[0, pl.ds(st, tq), sl] for st in starts], axis=0)
        qq = q_ref[0, :, sl]
        zero = jnp.zeros_like(qq)
        outs = []
        for hh in range(2):
            qx = jnp.where(lane < HALF_LANES, qq, zero) if hh == 0 else jnp.where(lane < HALF_LANES, zero, qq)
            s = _dot_nt(qx, kk) + bias_ref[2 * m + hh]
            s = jnp.where(key_ok, s, NEG)
            mx = jnp.max(s, axis=-1, keepdims=True)
            p = jnp.exp(s - mx)
            den = jnp.sum(p, axis=-1, keepdims=True)
            o = jnp.dot(p.astype(BF16), vv, preferred_element_type=F32)
            outs.append(o / den)
        o_ref[0, :, sl] = jnp.where(lane < HALF_LANES, outs[0], outs[1]).astype(BF16)


def _ca_call(q, k, v, bias):
    B, S, W = q.shape
    tq = TQ_BAND
    return pl.pallas_call(
        _ca_kernel,
        grid=(B, S // tq),
        in_specs=[pl.BlockSpec((1, tq, W), lambda b, i: (b, i, 0)),
                  pl.BlockSpec((1, S, W), lambda b, i: (b, 0, 0)),
                  pl.BlockSpec((1, S, W), lambda b, i: (b, 0, 0)),
                  pl.BlockSpec(bias.shape, lambda b, i: (0, 0, 0))],
        out_specs=pl.BlockSpec((1, tq, W), lambda b, i: (b, i, 0)),
        out_shape=jax.ShapeDtypeStruct(q.shape, BF16),
        compiler_params=_cparams(("parallel", "arbitrary")),
        name="chunk_attention",
    )(q, k, v, bias)


def _in1_kernel(x_ref, gmix_ref, win_ref, gq_ref, gk_ref, q_ref, k_ref, v_ref):
    xn = _rms(x_ref[...], gmix_ref[...]).astype(BF16)
    p = jnp.dot(xn, win_ref[...], preferred_element_type=F32)
    w = CA_HEADS * CA_HEAD_DIM
    for m in range(CA_HEADS // 2):
        sl = slice(m * LANES, (m + 1) * LANES)
        q_ref[:, sl] = _rms_two_heads(p[:, m * LANES:(m + 1) * LANES], gq_ref[...], CA_HEAD_DIM ** -0.5).astype(BF16)
        k_ref[:, sl] = _rms_two_heads(p[:, w + m * LANES:w + (m + 1) * LANES], gk_ref[...], 1.0).astype(BF16)
    v_ref[...] = p[:, 2 * w:3 * w].astype(BF16)


def _in1_call(x2d, gmix, win, gq, gk):
    T = x2d.shape[0]
    tm = TM_PROJ
    w = CA_HEADS * CA_HEAD_DIM
    row = lambda i: (i, 0)
    const = lambda i: (0, 0)
    full = lambda a: pl.BlockSpec(a.shape, const)
    out_shapes = tuple(jax.ShapeDtypeStruct((T, w), BF16) for _ in range(3))
    return pl.pallas_call(
        _in1_kernel,
        grid=(T // tm,),
        in_specs=[pl.BlockSpec((tm, D_MODEL), row), full(gmix), full(win), full(gq), full(gk)],
        out_specs=tuple(pl.BlockSpec((tm, w), row) for _ in range(3)),
        out_shape=out_shapes,
        compiler_params=_cparams(("parallel",)),
        name="layer1_in_proj",
    )(x2d, gmix, win, gq, gk)


def _out_kernel(x_ref, o_ref, w_ref, g_ref, xo_ref, xnt_ref):
    y = x_ref[...] + jnp.dot(o_ref[...], w_ref[...], preferred_element_type=F32)
    xo_ref[...] = y
    xnt_ref[...] = _rms(y, g_ref[...]).T.astype(BF16)


def _out_call(x2d, o2d, w, g):
    T = x2d.shape[0]
    tm = TM_PROJ
    row = lambda i: (i, 0)
    const = lambda i: (0, 0)
    return pl.pallas_call(
        _out_kernel,
        grid=(T // tm,),
        in_specs=[pl.BlockSpec((tm, D_MODEL), row), pl.BlockSpec((tm, o2d.shape[1]), row),
                  pl.BlockSpec(w.shape, const), pl.BlockSpec(g.shape, const)],
        out_specs=(pl.BlockSpec((tm, D_MODEL), row), pl.BlockSpec((D_MODEL, tm), lambda i: (0, i))),
        out_shape=(jax.ShapeDtypeStruct((T, D_MODEL), F32), jax.ShapeDtypeStruct((D_MODEL, T), BF16)),
        compiler_params=_cparams(("parallel",)),
        name="out_proj_residual",
    )(x2d, o2d, w, g)


NTB_PEER = TQ_PEER // LANES
I1_PER_STEP = EB_PEER // PEER_N_KEYS


SUBLANES = 8
BIG = 3.0e38

SORT16 = (
    (0, 13), (1, 12), (2, 15), (3, 14), (4, 8), (5, 6), (7, 11), (9, 10),
    (0, 5), (1, 7), (2, 9), (3, 4), (6, 13), (8, 14), (10, 15), (11, 12),
    (0, 1), (2, 3), (4, 5), (6, 8), (7, 9), (10, 11), (12, 13), (14, 15),
    (0, 2), (1, 3), (4, 10), (5, 11), (6, 7), (8, 9), (12, 14), (13, 15),
    (1, 2), (3, 12), (4, 6), (5, 7), (8, 10), (9, 11), (13, 14),
    (1, 4), (2, 6), (5, 8), (7, 10), (9, 13), (11, 14),
    (2, 4), (3, 6), (9, 12), (11, 13),
    (3, 5), (6, 8), (7, 9), (10, 12),
    (3, 4), (5, 6), (7, 8), (9, 10), (11, 12),
    (6, 7), (8, 9),
)
BITONIC16 = tuple((r, r + st) for st in (8, 4, 2, 1) for r in range(16) if (r // st) % 2 == 0)


def _compare_exchange(v, i, j):
    hi, lo = jnp.maximum(v[i], v[j]), jnp.minimum(v[i], v[j])
    v[i], v[j] = hi, lo


def _merge_across_sublanes(v):
    for shift in (4, 2, 1):
        y = [pltpu.roll(t, shift, 0) for t in v]
        v = [jnp.maximum(v[r], y[PEER_TOPK - 1 - r]) for r in range(PEER_TOPK)]
        for i, j in BITONIC16:
            _compare_exchange(v, i, j)
    return v


def _top16(x):
    v = [x[SUBLANES * i:SUBLANES * (i + 1), :] for i in range(PEER_N_KEYS // SUBLANES)]
    for i, j in SORT16:
        _compare_exchange(v, i, j)
    return _merge_across_sublanes(v)


def _route_tile(s1, s2):
    ta = _top16(s1)
    tb = _top16(s2)
    sub = lax.broadcasted_iota(jnp.int32, (SUBLANES, LANES), 0)
    a8 = ta[SUBLANES - 1]
    for j in range(SUBLANES - 2, -1, -1):
        a8 = jnp.where(sub == j, ta[j], a8)
    cl = []
    for k in range(PEER_TOPK):
        nj = PEER_TOPK // (k + 1)
        l = a8 + tb[k]
        cl.append(l if nj >= SUBLANES else jnp.where(sub < nj, l, NEG))
    m = _merge_across_sublanes(list(cl))
    ex = [ta[SUBLANES + r] + tb[0] for r in range(SUBLANES)]
    top = m[:SUBLANES] + [jnp.maximum(m[r], ex[PEER_TOPK - 1 - r]) for r in range(SUBLANES, PEER_TOPK)]
    for i, j in BITONIC16:
        _compare_exchange(top, i, j)
    tau = top[PEER_TOPK - 1]
    z = jnp.exp(top[0] - top[0])
    for r in range(1, PEER_TOPK):
        z = z + jnp.exp(top[r] - top[0])
    rz = 0.5 / z

    e2k = [_bf16_bits(jnp.exp(tb[k] - tb[0])) for k in range(PEER_TOPK)]
    e1top = _bf16_bits(jnp.exp(a8 - ta[0]) * rz)
    min_sel = jnp.full((SUBLANES, LANES), 0x7F000000, jnp.int32)
    max_unsel = jnp.zeros((SUBLANES, LANES), jnp.int32)

    def fold(min_sel, max_unsel, sel, prod):
        return (jnp.minimum(min_sel, jnp.where(sel, prod, 0x7F000000)),
                jnp.maximum(max_unsel, jnp.where(sel, 0, prod)))

    for k in range(PEER_TOPK):
        nj = PEER_TOPK // (k + 1)
        prod = _bf16_product_bits(e1top, e2k[k])
        if nj < SUBLANES:
            prod = jnp.where(sub < nj, prod, 0)
        min_sel, max_unsel = fold(min_sel, max_unsel, cl[k] >= tau, prod)
    for r in range(SUBLANES):
        e1x = _bf16_bits(jnp.exp(ta[SUBLANES + r] - ta[0]) * rz)
        min_sel, max_unsel = fold(min_sel, max_unsel, ex[r] >= tau, _bf16_product_bits(e1x, e2k[0]))
    for shift in (4, 2, 1):
        min_sel = jnp.minimum(min_sel, pltpu.roll(min_sel, shift, 0))
        max_unsel = jnp.maximum(max_unsel, pltpu.roll(max_unsel, shift, 0))
    mid = 0.5 * (pltpu.bitcast(min_sel, F32) + pltpu.bitcast(max_unsel, F32))
    thr = (pltpu.bitcast(mid, jnp.int32) + 0xFFFF) & jnp.int32(-65536)
    n = PEER_N_KEYS // SUBLANES
    e1_rows, e2_rows = [], []
    for i in range(n):
        x1 = s1[SUBLANES * i:SUBLANES * (i + 1), :]
        x2 = s2[SUBLANES * i:SUBLANES * (i + 1), :]
        e1_rows.append(jnp.where(x1 >= ta[PEER_TOPK - 1], _bf16_bits(jnp.exp(x1 - ta[0]) * rz), 0))
        e2_rows.append(jnp.where(x2 >= tb[PEER_TOPK - 1], _bf16_bits(jnp.exp(x2 - tb[0])), 0))
    e1 = _dup_halves(jnp.concatenate(e1_rows, axis=0))
    e2 = pltpu.bitcast(jnp.concatenate(e2_rows, axis=0), F32).astype(BF16)
    return e1, e2, _dup_halves(thr)


def _bf16_bits(x):
    u = pltpu.bitcast(x, jnp.int32)
    return (u + 0x7FFF + ((u >> 16) & 1)) & jnp.int32(-65536)


def _bf16_product_bits(a_bits, b_bits):
    return _bf16_bits(pltpu.bitcast(a_bits, F32) * pltpu.bitcast(b_bits, F32))


def _dup_halves(bits):
    return bits | ((bits >> 16) & 0xFFFF)


def _bf16_rows(word_row):
    return pltpu.bitcast(jnp.broadcast_to(word_row, (PEER_N_KEYS // 2, LANES)), BF16)


def _peer_kernel(x_ref, xnt_ref, wqt_ref, sk_ref, u_ref, vt_ref, o_ref,
                 q_scr, sc_scr, e1_scr, e2_scr, thr_scr, g_scr, acc_scr):
    e = pl.program_id(1)

    @pl.when(e == 0)
    def _route():
        q_scr[...] = jnp.dot(wqt_ref[...], xnt_ref[...], preferred_element_type=F32).astype(BF16)
        acc_scr[...] = jnp.zeros_like(acc_scr)

        def per_head(h, _):
            for p in range(2):
                hp = h * 2 + p
                qs = q_scr[pl.ds(pl.multiple_of(hp * PEER_HALF, PEER_HALF), PEER_HALF), :]
                sc = jnp.dot(sk_ref[hp], qs, preferred_element_type=F32)
                for tb in range(NTB_PEER):
                    sc_scr[p, tb] = sc[:, tb * LANES:(tb + 1) * LANES]

            def per_tile(tb, _):
                e1, e2, thr = _route_tile(sc_scr[0, tb], sc_scr[1, tb])
                e1_scr[h, tb] = e1
                e2_scr[h, tb] = e2
                thr_scr[h, tb] = thr
                return 0

            lax.fori_loop(0, NTB_PEER, per_tile, 0)
            return 0

        lax.fori_loop(0, PEER_HEADS, per_head, 0)

    n_sub = TQ_PEER // TS_PEER
    n_ch = EB_PEER // CH_PEER
    d_slab = D_MODEL // n_ch

    def pre_act(s, c):
        return jnp.dot(u_ref[c * CH_PEER:(c + 1) * CH_PEER, :], xnt_ref[:, s * TS_PEER:(s + 1) * TS_PEER],
                       preferred_element_type=F32)

    def route_weights(s, c, a):
        for r2 in range(CH_PEER // PEER_N_KEYS):
            r = c * (CH_PEER // PEER_N_KEYS) + r2
            i1 = e * I1_PER_STEP + r
            for tb2 in range(TS_PEER // LANES):
                tb = s * (TS_PEER // LANES) + tb2
                wsum = jnp.zeros((PEER_N_KEYS, LANES), BF16)
                for h in range(PEER_HEADS):
                    e1 = _bf16_rows(e1_scr[h, tb, pl.ds(i1, 1), :])
                    w = e2_scr[h, tb] * e1
                    wsum = wsum + jnp.where(w >= _bf16_rows(thr_scr[h, tb, 0:1, :]), w, jnp.zeros_like(w))
                at = a[r2 * PEER_N_KEYS:(r2 + 1) * PEER_N_KEYS, tb2 * LANES:(tb2 + 1) * LANES]
                g = wsum.astype(F32) * at * (1.0 + jnp.tanh(at * (GELU_C0 + GELU_C1 * (at * at))))
                g_scr[s % 2, r * PEER_N_KEYS:(r + 1) * PEER_N_KEYS, tb2 * LANES:(tb2 + 1) * LANES] = g.astype(BF16)

    def value_slab(s, d):
        ds_ = slice(d * d_slab, (d + 1) * d_slab)
        cs = slice(s * TS_PEER, (s + 1) * TS_PEER)
        acc_scr[ds_, cs] += jnp.dot(vt_ref[ds_, :], g_scr[s % 2], preferred_element_type=F32)

    stages = [(s, c) for s in range(n_sub) for c in range(n_ch)]
    a_next = pre_act(*stages[0])
    for idx, (s, c) in enumerate(stages):
        a_cur = a_next
        if idx + 1 < len(stages):
            a_next = pre_act(*stages[idx + 1])
        route_weights(s, c, a_cur)
        if s > 0:
            value_slab(s - 1, c)
    for d in range(n_ch):
        value_slab(n_sub - 1, d)

    @pl.when(e == pl.num_programs(1) - 1)
    def _finish():
        o_ref[...] = x_ref[...] + acc_scr[...].T


def _peer_call(x2d, xnt, wqt, sk, u, vt):
    T = x2d.shape[0]
    tq, eb = TQ_PEER, EB_PEER
    n_exp = u.shape[0]
    return pl.pallas_call(
        _peer_kernel,
        grid=(T // tq, n_exp // eb),
        in_specs=[pl.BlockSpec((tq, D_MODEL), lambda t, e: (t, 0)),
                  pl.BlockSpec((D_MODEL, tq), lambda t, e: (0, t)),
                  pl.BlockSpec(wqt.shape, lambda t, e: (0, 0)),
                  pl.BlockSpec(sk.shape, lambda t, e: (0, 0, 0)),
                  pl.BlockSpec((eb, D_MODEL), lambda t, e: (e, 0)),
                  pl.BlockSpec((D_MODEL, eb), lambda t, e: (0, e))],
        out_specs=pl.BlockSpec((tq, D_MODEL), lambda t, e: (t, 0)),
        out_shape=jax.ShapeDtypeStruct((T, D_MODEL), F32),
        scratch_shapes=[
            pltpu.VMEM((2 * PEER_HEADS * PEER_HALF, tq), BF16),
            pltpu.VMEM((2, NTB_PEER, PEER_N_KEYS, LANES), F32),
            pltpu.VMEM((PEER_HEADS, NTB_PEER, PEER_N_KEYS, LANES), jnp.int32),
            pltpu.VMEM((PEER_HEADS, NTB_PEER, PEER_N_KEYS, LANES), BF16),
            pltpu.VMEM((PEER_HEADS, NTB_PEER, SUBLANES, LANES), jnp.int32),
            pltpu.VMEM((2, eb, TS_PEER), BF16),
            pltpu.VMEM((D_MODEL, tq), F32),
        ],
        compiler_params=_cparams(("parallel", "arbitrary")),
        name="peer_dense",
    )(x2d, xnt, wqt, sk, u, vt)


def _t5_bucket(rel):
    nb = T5_BUCKETS // 2
    max_exact = nb // 2
    ret = jnp.where(rel > 0, nb, 0)
    n = jnp.abs(rel)
    large = max_exact + (jnp.log(jnp.maximum(n, 1).astype(F32) / max_exact)
                         / math.log(T5_MAX_DIST / max_exact) * (nb - max_exact)).astype(jnp.int32)
    large = jnp.minimum(large, nb - 1)
    return ret + jnp.where(n < max_exact, n, large)


def _swa_bias(t5_table):
    tq = TQ_BAND
    r = jnp.arange(tq)[:, None]
    c = jnp.arange(2 * tq)[None, :]
    rel = c - tq - r
    bias = t5_table[_t5_bucket(rel)].astype(F32).transpose(2, 0, 1)
    qc = (tq + r) // CHUNK
    kc = c // CHUNK
    vis = jnp.logical_and(kc >= qc - SWA_BAND, kc <= qc)
    return jnp.where(vis[None], bias, NEG)


def _ca_bias(rel_table):
    tq = TQ_BAND
    left = CA_LEFT_CHUNKS * CHUNK
    r = jnp.arange(tq)[:, None]
    c = jnp.arange(left + tq)[None, :]
    rel = c - left - r
    idx = jnp.clip(rel, -CA_REL_PAST, CHUNK - 1) + CA_REL_PAST
    bias = rel_table[:, idx].astype(F32)
    qc = (left + r) // CHUNK
    kc = c // CHUNK
    vis = jnp.logical_and(kc >= qc - CA_LEFT_CHUNKS, kc <= qc)
    return jnp.where(vis[None], bias, NEG)


def _rope_tables(seq):
    half = MLA_ROPE // 2
    inv = ROPE_THETA ** (-jnp.arange(0, MLA_ROPE, 2, dtype=F32) / MLA_ROPE)
    ang = jnp.arange(seq, dtype=F32)[:, None] * inv[None, :]
    cos, sin = jnp.cos(ang), jnp.sin(ang)
    z = lambda n: jnp.zeros((seq, n), F32)
    tail = LANES - MLA_QK
    rc = jnp.concatenate([jnp.ones((seq, MLA_NOPE), F32), cos, cos, z(tail)], axis=1)
    rs1 = jnp.concatenate([z(MLA_NOPE), -sin, z(half), z(tail)], axis=1)
    rs2 = jnp.concatenate([z(MLA_NOPE), z(half), sin, z(tail)], axis=1)
    return rc, rs1, rs2


SWA_HEAD_ORDER = tuple(h for m in range(SWA_GROUP) for h in (m, m + SWA_GROUP))


def _layer0_weights(ev_w_in, ev_w_out, mla_w_uq, mla_w_ukv, mla_g_q, mla_g_k, swa_g_q, swa_g_k):
    pad_head = LANES - MLA_QK
    mla_in = MLA_Q_RANK + MLA_KV_RANK + MLA_ROPE
    nq = SWA_HEADS * SWA_HEAD_DIM
    nkv = SWA_KV_HEADS * SWA_HEAD_DIM
    order = jnp.asarray(SWA_HEAD_ORDER)
    w_cq_ckv = ev_w_in[:, :MLA_Q_RANK + MLA_KV_RANK]
    w_kpe = ev_w_in[:, MLA_Q_RANK + MLA_KV_RANK:mla_in]
    w_sq = ev_w_in[:, mla_in:mla_in + nq].reshape(D_MODEL, SWA_HEADS, SWA_HEAD_DIM)[:, order].reshape(D_MODEL, nq)
    w_skv = ev_w_in[:, mla_in + nq:mla_in + nq + 2 * nkv]
    win = jnp.concatenate([w_cq_ckv, w_sq, w_skv, w_kpe, jnp.zeros((D_MODEL, LANES - MLA_ROPE), F32)], axis=1)
    wq = jnp.pad(mla_w_uq, ((0, 0), (0, 0), (0, pad_head))).reshape(MLA_Q_RANK, MLA_HEADS * LANES)
    wkc = jnp.pad(mla_w_ukv[:, :, :MLA_NOPE], ((0, 0), (0, 0), (0, LANES - MLA_NOPE))).reshape(MLA_KV_RANK, MLA_HEADS * LANES)
    place = jnp.pad(jnp.eye(MLA_ROPE, dtype=F32), ((0, LANES - MLA_ROPE), (MLA_NOPE, pad_head)))
    wkp = jnp.tile(place, (1, MLA_HEADS))
    wv = mla_w_ukv[:, :, MLA_NOPE:].reshape(MLA_KV_RANK, MLA_HEADS * MLA_V)
    gq = jnp.pad(mla_g_q, (0, pad_head)).reshape(1, LANES)
    gk = jnp.pad(mla_g_k, (0, pad_head)).reshape(1, LANES)
    sgq = jnp.tile(swa_g_q, 2).reshape(1, LANES)
    sgk = jnp.tile(swa_g_k, 2).reshape(1, LANES)
    n_a = MLA_HEADS * MLA_V
    w_out_b = ev_w_out[n_a:].reshape(SWA_HEADS, SWA_HEAD_DIM, D_MODEL)[order].reshape(nq, D_MODEL)
    wout = jnp.concatenate([ev_w_out[:n_a], w_out_b], axis=0)
    return dict(win=win.astype(BF16), wq=wq.astype(BF16), wkc=wkc.astype(BF16), wkp=wkp.astype(BF16),
                wv=wv.astype(BF16), gq=gq, gk=gk, sgq=sgq, sgk=sgk, wout=wout.astype(BF16))


def _peer_layer(x2d, xnt, w_query, sub_keys, u_emb, v_emb):
    wqt = w_query.reshape(D_MODEL, 2 * PEER_HEADS * PEER_HALF).T.astype(BF16)
    sk = sub_keys.reshape(2 * PEER_HEADS, PEER_N_KEYS, PEER_HALF).astype(BF16)
    return _peer_call(x2d, xnt, wqt, sk, u_emb.astype(BF16), v_emb.T.astype(BF16))


def kernel(x, t5_bias, norm_mix, norm_ffn, ev_w_in, ev_w_out, mla_g_cq, mla_w_uq, mla_g_ckv, mla_w_ukv,
           mla_g_q, mla_g_k, swa_g_q, swa_g_k, swa_sinks, od_w_in, od_w_out, ca_g_q, ca_g_k, ca_rel_bias,
           peer_w_query, peer_sub_keys, peer_u, peer_v):
    B, S, D = x.shape
    T = B * S
    x2d = x.reshape(T, D)
    row = lambda a: a.reshape(1, -1)

    w0 = _layer0_weights(ev_w_in[0], ev_w_out[0], mla_w_uq[0], mla_w_ukv[0], mla_g_q[0], mla_g_k[0],
                         swa_g_q[0], swa_g_k[0])
    rc, rs1, rs2 = _rope_tables(S)
    qm, km, vm, qs, ks, vs = _in0_call(
        x2d, S, row(norm_mix[0]), w0["win"], row(mla_g_cq[0]), w0["wq"], w0["gq"], row(mla_g_ckv[0]),
        w0["wkc"], w0["wkp"], w0["gk"], w0["wv"], w0["sgq"], w0["sgk"], rc, rs1, rs2)
    b3 = lambda a: a.reshape(B, S, a.shape[1])
    o_a = _mla_call(b3(qm), b3(km), b3(vm))
    o_b = _swa_call(swa_sinks[0], b3(qs), b3(ks), b3(vs), _swa_bias(t5_bias))
    o0 = jnp.concatenate([o_a, o_b], axis=-1).reshape(T, -1)
    x2d, xnt = _out_call(x2d, o0, w0["wout"], row(norm_ffn[0]))
    x2d = _peer_layer(x2d, xnt, peer_w_query[0], peer_sub_keys[0], peer_u[0], peer_v[0])

    gq2 = jnp.tile(ca_g_q[0], 2).reshape(1, LANES)
    gk2 = jnp.tile(ca_g_k[0], 2).reshape(1, LANES)
    qc, kc, vc = _in1_call(x2d, row(norm_mix[1]), od_w_in[0].astype(BF16), gq2, gk2)
    o_c = _ca_call(b3(qc), b3(kc), b3(vc), _ca_bias(ca_rel_bias[0]))
    x2d, xnt = _out_call(x2d, o_c.reshape(T, -1), od_w_out[0].astype(BF16), row(norm_ffn[1]))
    x2d = _peer_layer(x2d, xnt, peer_w_query[1], peer_sub_keys[1], peer_u[1], peer_v[1])
    return x2d.reshape(B, S, D)
```

```python
import functools
import math

import jax
import jax.numpy as jnp
import numpy as np
from jax import lax
from jax.experimental import pallas as pl
from jax.experimental.pallas import tpu as pltpu

F32 = jnp.float32
BF16 = jnp.bfloat16

D_MODEL = 1024
CHUNK = 64
EPS = 1e-6
NEG = -1e30
LANES = 128
HALF_LANES = LANES // 2

T5_BUCKETS = 32
T5_MAX_DIST = 128

MLA_HEADS = 8
MLA_NOPE = 64
MLA_ROPE = 32
MLA_V = 64
MLA_QK = MLA_NOPE + MLA_ROPE
MLA_Q_RANK = 384
MLA_KV_RANK = 256
ROPE_THETA = 10000.0

SWA_HEADS = 8
SWA_KV_HEADS = 2
SWA_GROUP = SWA_HEADS // SWA_KV_HEADS
SWA_HEAD_DIM = 64
SWA_BAND = 2

CA_HEADS = 16
CA_HEAD_DIM = 64
CA_LEFT_CHUNKS = 8
CA_REL_PAST = 256

PEER_HEADS = 8
PEER_N_KEYS = 128
PEER_HALF = 128
PEER_TOPK = 16

VMEM_LIMIT = 56 * 1024 * 1024

TM_PROJ = 256
TQ_MLA = 256
TQ_BAND = 128
TQ_PEER = 512
EB_PEER = 1024
TS_PEER = 256
QR_PEER = 16

GELU_C0 = math.sqrt(2.0 / math.pi)
GELU_C1 = GELU_C0 * 0.044715


def _cparams(sem):
    return pltpu.CompilerParams(dimension_semantics=sem, vmem_limit_bytes=VMEM_LIMIT)


def _rms(x, g):
    ms = jnp.mean(x * x, axis=-1, keepdims=True)
    return x * lax.rsqrt(ms + EPS) * g


def _rms_two_heads(x, g2, scale):
    lane = lax.broadcasted_iota(jnp.int32, x.shape, 1)
    lo_mask = lane < HALF_LANES
    s = x * x
    lo = jnp.sum(jnp.where(lo_mask, s, 0.0), axis=-1, keepdims=True)
    hi = jnp.sum(jnp.where(lo_mask, 0.0, s), axis=-1, keepdims=True)
    r = jnp.where(lo_mask, lax.rsqrt(lo * (1.0 / HALF_LANES) + EPS), lax.rsqrt(hi * (1.0 / HALF_LANES) + EPS))
    return x * r * (g2 * scale)


def _dot_nt(a, b):
    return lax.dot_general(a, b, (((1,), (1,)), ((), ())), preferred_element_type=F32)


def _in0_kernel(x_ref, gmix_ref, win_ref, gcq_ref, wq_ref, gq_ref, gckv_ref, wkc_ref, wkp_ref, gk_ref,
                wv_ref, sgq_ref, sgk_ref, rc_ref, rs1_ref, rs2_ref,
                qm_ref, km_ref, vm_ref, qs_ref, ks_ref, vs_ref):
    xn = _rms(x_ref[...], gmix_ref[...]).astype(BF16)
    p = jnp.dot(xn, win_ref[...], preferred_element_type=F32)
    cq = _rms(p[:, 0:MLA_Q_RANK], gcq_ref[...]).astype(BF16)
    ckv = _rms(p[:, 384:640], gckv_ref[...]).astype(BF16)
    kpe = p[:, 1408:1536].astype(BF16)
    q = jnp.dot(cq, wq_ref[...], preferred_element_type=F32)
    k = (jnp.dot(ckv, wkc_ref[...], preferred_element_type=F32)
         + jnp.dot(kpe, wkp_ref[...], preferred_element_type=F32))
    vm_ref[...] = jnp.dot(ckv, wv_ref[...], preferred_element_type=F32).astype(BF16)

    rc, rs1, rs2 = rc_ref[...], rs1_ref[...], rs2_ref[...]

    def head_norm_rope(t, g, scale):
        ms = jnp.sum(t * t, axis=-1, keepdims=True) * (1.0 / MLA_QK)
        tn = t * lax.rsqrt(ms + EPS) * g
        tr = tn * rc + pltpu.roll(tn, LANES - MLA_ROPE // 2, 1) * rs1 + pltpu.roll(tn, MLA_ROPE // 2, 1) * rs2
        return (tr * scale).astype(BF16)

    for h in range(MLA_HEADS):
        sl = slice(h * LANES, (h + 1) * LANES)
        qm_ref[:, sl] = head_norm_rope(q[:, sl], gq_ref[...], MLA_QK ** -0.5)
        km_ref[:, sl] = head_norm_rope(k[:, sl], gk_ref[...], 1.0)

    for m in range(SWA_HEADS // 2):
        sl = slice(640 + m * LANES, 640 + (m + 1) * LANES)
        qs_ref[:, m * LANES:(m + 1) * LANES] = _rms_two_heads(p[:, sl], sgq_ref[...], SWA_HEAD_DIM ** -0.5).astype(BF16)
    ks_ref[...] = _rms_two_heads(p[:, 1152:1280], sgk_ref[...], 1.0).astype(BF16)
    vs_ref[...] = p[:, 1280:1408].astype(BF16)


def _in0_call(x2d, seq, gmix, win, gcq, wq, gq, gckv, wkc, wkp, gk, wv, sgq, sgk, rc, rs1, rs2):
    T = x2d.shape[0]
    tm = TM_PROJ
    nblk_seq = seq // tm
    row = lambda i: (i, 0)
    const = lambda i: (0, 0)
    pos = lambda i: (i % nblk_seq, 0)
    full = lambda a: pl.BlockSpec(a.shape, const)
    out_shapes = (
        jax.ShapeDtypeStruct((T, MLA_HEADS * LANES), BF16),
        jax.ShapeDtypeStruct((T, MLA_HEADS * LANES), BF16),
        jax.ShapeDtypeStruct((T, MLA_HEADS * MLA_V), BF16),
        jax.ShapeDtypeStruct((T, SWA_HEADS * SWA_HEAD_DIM), BF16),
        jax.ShapeDtypeStruct((T, SWA_KV_HEADS * SWA_HEAD_DIM), BF16),
        jax.ShapeDtypeStruct((T, SWA_KV_HEADS * SWA_HEAD_DIM), BF16),
    )
    return pl.pallas_call(
        _in0_kernel,
        grid=(T // tm,),
        in_specs=[pl.BlockSpec((tm, D_MODEL), row), full(gmix), full(win), full(gcq), full(wq), full(gq),
                  full(gckv), full(wkc), full(wkp), full(gk), full(wv), full(sgq), full(sgk),
                  pl.BlockSpec((tm, LANES), pos), pl.BlockSpec((tm, LANES), pos), pl.BlockSpec((tm, LANES), pos)],
        out_specs=tuple(pl.BlockSpec((tm, s.shape[1]), row) for s in out_shapes),
        out_shape=out_shapes,
        compiler_params=_cparams(("parallel",)),
        name="layer0_in_proj",
    )(x2d, gmix, win, gcq, wq, gq, gckv, wkc, wkp, gk, wv, sgq, sgk, rc, rs1, rs2)


def _mla_kernel(q_ref, k_ref, v_ref, o_ref):
    i = pl.program_id(1)
    tq = TQ_MLA
    rows = lax.broadcasted_iota(jnp.int32, (tq, tq), 0)
    cols = lax.broadcasted_iota(jnp.int32, (tq, tq), 1)
    diag_ok = (cols // CHUNK) <= (rows // CHUNK)
    lane = lax.broadcasted_iota(jnp.int32, (tq, LANES), 1)
    for hp in range(MLA_HEADS // 2):
        outs = []
        for hh in range(2):
            h = 2 * hp + hh
            q = q_ref[0, :, h * LANES:(h + 1) * LANES]

            def body(j, carry, q=q, h=h, hp=hp):
                m, l, acc = carry
                start = pl.multiple_of(j * tq, tq)
                kb = k_ref[0, pl.ds(start, tq), h * LANES:(h + 1) * LANES]
                vb = v_ref[0, pl.ds(start, tq), hp * LANES:(hp + 1) * LANES]
                s = _dot_nt(q, kb)
                s = jnp.where(jnp.logical_or(j < i, diag_ok), s, NEG)
                m_new = jnp.maximum(m, jnp.max(s, axis=-1, keepdims=True))
                alpha = jnp.exp(m - m_new)
                p = jnp.exp(s - m_new)
                l = alpha * l + jnp.sum(p, axis=-1, keepdims=True)
                acc = alpha * acc + jnp.dot(p.astype(BF16), vb, preferred_element_type=F32)
                return m_new, l, acc

            init = (jnp.full((tq, 1), NEG, F32), jnp.zeros((tq, 1), F32), jnp.zeros((tq, LANES), F32))
            m, l, acc = lax.fori_loop(0, i + 1, body, init)
            outs.append(acc / l)
        o_ref[0, :, hp * LANES:(hp + 1) * LANES] = jnp.where(lane < HALF_LANES, outs[0], outs[1]).astype(BF16)


def _mla_call(q, k, v):
    B, S, _ = q.shape
    tq = TQ_MLA
    return pl.pallas_call(
        _mla_kernel,
        grid=(B, S // tq),
        in_specs=[pl.BlockSpec((1, tq, q.shape[2]), lambda b, i: (b, i, 0)),
                  pl.BlockSpec((1, S, k.shape[2]), lambda b, i: (b, 0, 0)),
                  pl.BlockSpec((1, S, v.shape[2]), lambda b, i: (b, 0, 0))],
        out_specs=pl.BlockSpec((1, tq, v.shape[2]), lambda b, i: (b, i, 0)),
        out_shape=jax.ShapeDtypeStruct((B, S, v.shape[2]), BF16),
        compiler_params=_cparams(("parallel", "arbitrary")),
        name="latent_attention",
    )(q, k, v)


def _swa_kernel(sink_ref, q_ref, kp_ref, kc_ref, vp_ref, vc_ref, bias_ref, o_ref):
    i = pl.program_id(1)
    tq = TQ_BAND
    kk = jnp.concatenate([kp_ref[0], kc_ref[0]], axis=0)
    vv = jnp.concatenate([vp_ref[0], vc_ref[0]], axis=0)
    lane = lax.broadcasted_iota(jnp.int32, (tq, LANES), 1)
    col = lax.broadcasted_iota(jnp.int32, (tq, 2 * tq), 1)
    key_ok = jnp.logical_or(col >= tq, i > 0)
    for m in range(SWA_HEADS // 2):
        qq = q_ref[0, :, m * LANES:(m + 1) * LANES]
        zero = jnp.zeros_like(qq)
        outs = []
        for hh in range(2):
            h = m + hh * SWA_GROUP
            qx = jnp.where(lane < HALF_LANES, qq, zero) if hh == 0 else jnp.where(lane < HALF_LANES, zero, qq)
            s = _dot_nt(qx, kk) + bias_ref[h]
            s = jnp.where(key_ok, s, NEG)
            sink = sink_ref[h]
            mx = jnp.maximum(jnp.max(s, axis=-1, keepdims=True), sink)
            p = jnp.exp(s - mx)
            den = jnp.sum(p, axis=-1, keepdims=True) + jnp.exp(sink - mx)
            o = jnp.dot(p.astype(BF16), vv, preferred_element_type=F32)
            outs.append(o / den)
        o_ref[0, :, m * LANES:(m + 1) * LANES] = jnp.where(lane < HALF_LANES, outs[0], outs[1]).astype(BF16)


def _swa_call(sinks, q, k, v, bias):
    B, S, _ = q.shape
    tq = TQ_BAND
    prev = lambda b, i: (b, jnp.maximum(i - 1, 0), 0)
    cur = lambda b, i: (b, i, 0)
    return pl.pallas_call(
        _swa_kernel,
        grid=(B, S // tq),
        in_specs=[pl.BlockSpec(memory_space=pltpu.SMEM),
                  pl.BlockSpec((1, tq, q.shape[2]), cur),
                  pl.BlockSpec((1, tq, LANES), prev), pl.BlockSpec((1, tq, LANES), cur),
                  pl.BlockSpec((1, tq, LANES), prev), pl.BlockSpec((1, tq, LANES), cur),
                  pl.BlockSpec(bias.shape, lambda b, i: (0, 0, 0))],
        out_specs=pl.BlockSpec((1, tq, q.shape[2]), cur),
        out_shape=jax.ShapeDtypeStruct(q.shape, BF16),
        compiler_params=_cparams(("parallel", "arbitrary")),
        name="sliding_window_attention",
    )(sinks, q, k, k, v, v, bias)


CA_KBLOCKS = CA_LEFT_CHUNKS * CHUNK // TQ_BAND + 1


def _ca_kernel(q_ref, k_ref, v_ref, bias_ref, o_ref):
    i = pl.program_id(1)
    tq = TQ_BAND
    nk = CA_KBLOCKS * tq
    lane = lax.broadcasted_iota(jnp.int32, (tq, LANES), 1)
    col = lax.broadcasted_iota(jnp.int32, (tq, nk), 1)
    key_ok = (col // tq) >= (CA_KBLOCKS - 1 - i)
    starts = [pl.multiple_of(jnp.maximum(i - (CA_KBLOCKS - 1) + j, 0) * tq, tq) for j in range(CA_KBLOCKS)]
    for m in range(CA_HEADS // 2):
        sl = slice(m * LANES, (m + 1) * LANES)
        kk = jnp.concatenate([k_ref[0, pl.ds(st, tq), sl] for st in starts], axis=0)
        vv = jnp.concatenate([v_ref[0, pl.ds(st, tq), sl] for st in starts], axis=0)
        qq = q_ref[0, :, sl]
        zero = jnp.zeros_like(qq)
        outs = []
        for hh in range(2):
            qx = jnp.where(lane < HALF_LANES, qq, zero) if hh == 0 else jnp.where(lane < HALF_LANES, zero, qq)
            s = _dot_nt(qx, kk) + bias_ref[2 * m + hh]
            s = jnp.where(key_ok, s, NEG)
            mx = jnp.max(s, axis=-1, keepdims=True)
            p = jnp.exp(s - mx)
            den = jnp.sum(p, axis=-1, keepdims=True)
            o = jnp.dot(p.astype(BF16), vv, preferred_element_type=F32)
            outs.append(o / den)
        o_ref[0, :, sl] = jnp.where(lane < HALF_LANES, outs[0], outs[1]).astype(BF16)


def _ca_call(q, k, v, bias):
    B, S, W = q.shape
    tq = TQ_BAND
    return pl.pallas_call(
        _ca_kernel,
        grid=(B, S // tq),
        in_specs=[pl.BlockSpec((1, tq, W), lambda b, i: (b, i, 0)),
                  pl.BlockSpec((1, S, W), lambda b, i: (b, 0, 0)),
                  pl.BlockSpec((1, S, W), lambda b, i: (b, 0, 0)),
                  pl.BlockSpec(bias.shape, lambda b, i: (0, 0, 0))],
        out_specs=pl.BlockSpec((1, tq, W), lambda b, i: (b, i, 0)),
        out_shape=jax.ShapeDtypeStruct(q.shape, BF16),
        compiler_params=_cparams(("parallel", "arbitrary")),
        name="chunk_attention",
    )(q, k, v, bias)


def _in1_kernel(x_ref, gmix_ref, win_ref, gq_ref, gk_ref, q_ref, k_ref, v_ref):
    xn = _rms(x_ref[...], gmix_ref[...]).astype(BF16)
    p = jnp.dot(xn, win_ref[...], preferred_element_type=F32)
    w = CA_HEADS * CA_HEAD_DIM
    for m in range(CA_HEADS // 2):
        sl = slice(m * LANES, (m + 1) * LANES)
        q_ref[:, sl] = _rms_two_heads(p[:, m * LANES:(m + 1) * LANES], gq_ref[...], CA_HEAD_DIM ** -0.5).astype(BF16)
        k_ref[:, sl] = _rms_two_heads(p[:, w + m * LANES:w + (m + 1) * LANES], gk_ref[...], 1.0).astype(BF16)
    v_ref[...] = p[:, 2 * w:3 * w].astype(BF16)


def _in1_call(x2d, gmix, win, gq, gk):
    T = x2d.shape[0]
    tm = TM_PROJ
    w = CA_HEADS * CA_HEAD_DIM
    row = lambda i: (i, 0)
    const = lambda i: (0, 0)
    full = lambda a: pl.BlockSpec(a.shape, const)
    out_shapes = tuple(jax.ShapeDtypeStruct((T, w), BF16) for _ in range(3))
    return pl.pallas_call(
        _in1_kernel,
        grid=(T // tm,),
        in_specs=[pl.BlockSpec((tm, D_MODEL), row), full(gmix), full(win), full(gq), full(gk)],
        out_specs=tuple(pl.BlockSpec((tm, w), row) for _ in range(3)),
        out_shape=out_shapes,
        compiler_params=_cparams(("parallel",)),
        name="layer1_in_proj",
    )(x2d, gmix, win, gq, gk)


def _out_kernel(x_ref, o_ref, w_ref, g_ref, xo_ref, xnt_ref):
    y = x_ref[...] + jnp.dot(o_ref[...], w_ref[...], preferred_element_type=F32)
    xo_ref[...] = y
    xnt_ref[...] = _rms(y, g_ref[...]).T.astype(BF16)


def _out_call(x2d, o2d, w, g):
    T = x2d.shape[0]
    tm = TM_PROJ
    row = lambda i: (i, 0)
    const = lambda i: (0, 0)
    return pl.pallas_call(
        _out_kernel,
        grid=(T // tm,),
        in_specs=[pl.BlockSpec((tm, D_MODEL), row), pl.BlockSpec((tm, o2d.shape[1]), row),
                  pl.BlockSpec(w.shape, const), pl.BlockSpec(g.shape, const)],
        out_specs=(pl.BlockSpec((tm, D_MODEL), row), pl.BlockSpec((D_MODEL, tm), lambda i: (0, i))),
        out_shape=(jax.ShapeDtypeStruct((T, D_MODEL), F32), jax.ShapeDtypeStruct((D_MODEL, T), BF16)),
        compiler_params=_cparams(("parallel",)),
        name="out_proj_residual",
    )(x2d, o2d, w, g)


NTB_PEER = TQ_PEER // LANES
I1_PER_STEP = EB_PEER // PEER_N_KEYS


SUBLANES = 8
BIG = 3.0e38

SORT16 = (
    (0, 13), (1, 12), (2, 15), (3, 14), (4, 8), (5, 6), (7, 11), (9, 10),
    (0, 5), (1, 7), (2, 9), (3, 4), (6, 13), (8, 14), (10, 15), (11, 12),
    (0, 1), (2, 3), (4, 5), (6, 8), (7, 9), (10, 11), (12, 13), (14, 15),
    (0, 2), (1, 3), (4, 10), (5, 11), (6, 7), (8, 9), (12, 14), (13, 15),
    (1, 2), (3, 12), (4, 6), (5, 7), (8, 10), (9, 11), (13, 14),
    (1, 4), (2, 6), (5, 8), (7, 10), (9, 13), (11, 14),
    (2, 4), (3, 6), (9, 12), (11, 13),
    (3, 5), (6, 8), (7, 9), (10, 12),
    (3, 4), (5, 6), (7, 8), (9, 10), (11, 12),
    (6, 7), (8, 9),
)
BITONIC16 = tuple((r, r + st) for st in (8, 4, 2, 1) for r in range(16) if (r // st) % 2 == 0)


def _compare_exchange(v, i, j):
    hi, lo = jnp.maximum(v[i], v[j]), jnp.minimum(v[i], v[j])
    v[i], v[j] = hi, lo


def _merge_across_sublanes(v):
    for shift in (4, 2, 1):
        y = [pltpu.roll(t, shift, 0) for t in v]
        v = [jnp.maximum(v[r], y[PEER_TOPK - 1 - r]) for r in range(PEER_TOPK)]
        for i, j in BITONIC16:
            _compare_exchange(v, i, j)
    return v


def _top16(x):
    v = [x[SUBLANES * i:SUBLANES * (i + 1), :] for i in range(PEER_N_KEYS // SUBLANES)]
    for i, j in SORT16:
        _compare_exchange(v, i, j)
    return _merge_across_sublanes(v)


def _route_tile(s1, s2):
    ta = _top16(s1)
    tb = _top16(s2)
    sub = lax.broadcasted_iota(jnp.int32, (SUBLANES, LANES), 0)
    a8 = ta[SUBLANES - 1]
    for j in range(SUBLANES - 2, -1, -1):
        a8 = jnp.where(sub == j, ta[j], a8)
    cl = []
    for k in range(PEER_TOPK):
        nj = PEER_TOPK // (k + 1)
        l = a8 + tb[k]
        cl.append(l if nj >= SUBLANES else jnp.where(sub < nj, l, NEG))
    m = _merge_across_sublanes(list(cl))
    ex = [ta[SUBLANES + r] + tb[0] for r in range(SUBLANES)]
    top = m[:SUBLANES] + [jnp.maximum(m[r], ex[PEER_TOPK - 1 - r]) for r in range(SUBLANES, PEER_TOPK)]
    for i, j in BITONIC16:
        _compare_exchange(top, i, j)
    tau = top[PEER_TOPK - 1]
    z = jnp.exp(top[0] - top[0])
    for r in range(1, PEER_TOPK):
        z = z + jnp.exp(top[r] - top[0])
    rz = 0.5 / z

    e2k = [jnp.exp(tb[k] - tb[0]) for k in range(PEER_TOPK)]
    e1top = jnp.exp(a8 - ta[0]) * rz
    thr = jnp.full((SUBLANES, LANES), BIG, F32)
    for k in range(PEER_TOPK):
        thr = jnp.minimum(thr, jnp.where(cl[k] >= tau, e1top * e2k[k], BIG))
    for r in range(SUBLANES):
        e1x = jnp.exp(ta[SUBLANES + r] - ta[0]) * rz
        thr = jnp.minimum(thr, jnp.where(ex[r] >= tau, e1x * e2k[0], BIG))
    for shift in (4, 2, 1):
        thr = jnp.minimum(thr, pltpu.roll(thr, shift, 0))
    n = PEER_N_KEYS // SUBLANES
    e1 = jnp.concatenate([jnp.exp(s1[SUBLANES * i:SUBLANES * (i + 1), :] - ta[0]) * rz for i in range(n)], axis=0)
    e2 = jnp.concatenate([jnp.exp(s2[SUBLANES * i:SUBLANES * (i + 1), :] - tb[0]) for i in range(n)], axis=0)
    return e1, e2, thr


def _peer_kernel(x_ref, xnt_ref, wqt_ref, sk_ref, u_ref, vt_ref, o_ref,
                 q_scr, sc_scr, e1_scr, e2_scr, thr_scr, a_scr, g_scr, acc_scr):
    e = pl.program_id(1)

    @pl.when(e == 0)
    def _route():
        q_scr[...] = jnp.dot(wqt_ref[...], xnt_ref[...], preferred_element_type=F32).astype(BF16)
        acc_scr[...] = jnp.zeros_like(acc_scr)

        def per_head(h, _):
            for p in range(2):
                hp = h * 2 + p
                qs = q_scr[pl.ds(pl.multiple_of(hp * PEER_HALF, PEER_HALF), PEER_HALF), :]
                sc = jnp.dot(sk_ref[hp], qs, preferred_element_type=F32)
                for tb in range(NTB_PEER):
                    sc_scr[p, tb] = sc[:, tb * LANES:(tb + 1) * LANES]

            def per_tile(tb, _):
                e1, e2, thr = _route_tile(sc_scr[0, tb], sc_scr[1, tb])
                e1_scr[h, tb] = e1
                e2_scr[h, tb] = e2
                thr_scr[h, tb] = thr
                return 0

            lax.fori_loop(0, NTB_PEER, per_tile, 0)
            return 0

        lax.fori_loop(0, PEER_HEADS, per_head, 0)

    n_sub = TQ_PEER // TS_PEER

    def route_weights(s):
        for tb2 in range(TS_PEER // LANES):
            tb = s * (TS_PEER // LANES) + tb2
            ts = slice(tb2 * LANES, (tb2 + 1) * LANES)
            for q in range(PEER_N_KEYS // QR_PEER):
                wsum = [jnp.zeros((QR_PEER, LANES), F32) for _ in range(I1_PER_STEP)]
                for h in range(PEER_HEADS):
                    e2q = jnp.maximum(e2_scr[h, tb, q * QR_PEER:(q + 1) * QR_PEER, :], 0.0)
                    thr = thr_scr[h, tb, 0:1, :]
                    for r in range(I1_PER_STEP):
                        w = e2q * e1_scr[h, tb, pl.ds(e * I1_PER_STEP + r, 1), :]
                        wsum[r] = wsum[r] + jnp.where(w >= thr, w, 0.0)
                for r in range(I1_PER_STEP):
                    rows = slice(r * PEER_N_KEYS + q * QR_PEER, r * PEER_N_KEYS + (q + 1) * QR_PEER)
                    at = a_scr[s, rows, ts]
                    g = wsum[r] * at * (1.0 + jnp.tanh(at * (GELU_C0 + GELU_C1 * (at * at))))
                    g_scr[s, rows, ts] = g.astype(BF16)

    for s in range(n_sub):
        a_scr[s] = jnp.dot(u_ref[...], xnt_ref[:, s * TS_PEER:(s + 1) * TS_PEER],
                           preferred_element_type=F32)
    for s in range(n_sub):
        route_weights(s)
        acc_scr[:, s * TS_PEER:(s + 1) * TS_PEER] += jnp.dot(vt_ref[...], g_scr[s], preferred_element_type=F32)

    @pl.when(e == pl.num_programs(1) - 1)
    def _finish():
        o_ref[...] = x_ref[...] + acc_scr[...].T


def _peer_call(x2d, xnt, wqt, sk, u, vt):
    T = x2d.shape[0]
    tq, eb = TQ_PEER, EB_PEER
    n_exp = u.shape[0]
    return pl.pallas_call(
        _peer_kernel,
        grid=(T // tq, n_exp // eb),
        in_specs=[pl.BlockSpec((tq, D_MODEL), lambda t, e: (t, 0)),
                  pl.BlockSpec((D_MODEL, tq), lambda t, e: (0, t)),
                  pl.BlockSpec(wqt.shape, lambda t, e: (0, 0)),
                  pl.BlockSpec(sk.shape, lambda t, e: (0, 0, 0)),
                  pl.BlockSpec((eb, D_MODEL), lambda t, e: (e, 0)),
                  pl.BlockSpec((D_MODEL, eb), lambda t, e: (0, e))],
        out_specs=pl.BlockSpec((tq, D_MODEL), lambda t, e: (t, 0)),
        out_shape=jax.ShapeDtypeStruct((T, D_MODEL), F32),
        scratch_shapes=[
            pltpu.VMEM((2 * PEER_HEADS * PEER_HALF, tq), BF16),
            pltpu.VMEM((2, NTB_PEER, PEER_N_KEYS, LANES), F32),
            pltpu.VMEM((PEER_HEADS, NTB_PEER, PEER_N_KEYS, LANES), F32),
            pltpu.VMEM((PEER_HEADS, NTB_PEER, PEER_N_KEYS, LANES), F32),
            pltpu.VMEM((PEER_HEADS, NTB_PEER, SUBLANES, LANES), F32),
            pltpu.VMEM((tq // TS_PEER, eb, TS_PEER), F32),
            pltpu.VMEM((tq // TS_PEER, eb, TS_PEER), BF16),
            pltpu.VMEM((D_MODEL, tq), F32),
        ],
        compiler_params=_cparams(("parallel", "arbitrary")),
        name="peer_dense",
    )(x2d, xnt, wqt, sk, u, vt)


def _t5_bucket(rel):
    nb = T5_BUCKETS // 2
    max_exact = nb // 2
    ret = jnp.where(rel > 0, nb, 0)
    n = jnp.abs(rel)
    large = max_exact + (jnp.log(jnp.maximum(n, 1).astype(F32) / max_exact)
                         / math.log(T5_MAX_DIST / max_exact) * (nb - max_exact)).astype(jnp.int32)
    large = jnp.minimum(large, nb - 1)
    return ret + jnp.where(n < max_exact, n, large)


def _toeplitz(f, n_rows, n_cols):
    return jnp.stack([f[:, n_rows - 1 - r:n_rows - 1 - r + n_cols] for r in range(n_rows)], axis=1)


def _swa_bias(t5_table):
    tq = TQ_BAND
    r = jnp.arange(tq)[:, None]
    c = jnp.arange(2 * tq)[None, :]
    rel_line = jnp.arange(3 * tq - 1) - (2 * tq - 1)
    bias = _toeplitz(t5_table[_t5_bucket(rel_line)].astype(F32).T, tq, 2 * tq)
    qc = (tq + r) // CHUNK
    kc = c // CHUNK
    vis = jnp.logical_and(kc >= qc - SWA_BAND, kc <= qc)
    return jnp.where(vis[None], bias, NEG)


def _ca_bias(rel_table):
    tq = TQ_BAND
    left = CA_LEFT_CHUNKS * CHUNK
    r = jnp.arange(tq)[:, None]
    c = jnp.arange(left + tq)[None, :]
    rel_line = jnp.arange(left + 2 * tq - 1) - (left + tq - 1)
    idx = jnp.clip(rel_line, -CA_REL_PAST, CHUNK - 1) + CA_REL_PAST
    bias = _toeplitz(rel_table[:, idx].astype(F32), tq, left + tq)
    qc = (left + r) // CHUNK
    kc = c // CHUNK
    vis = jnp.logical_and(kc >= qc - CA_LEFT_CHUNKS, kc <= qc)
    return jnp.where(vis[None], bias, NEG)


def _rope_tables(seq):
    half = MLA_ROPE // 2
    inv = ROPE_THETA ** (-jnp.arange(0, MLA_ROPE, 2, dtype=F32) / MLA_ROPE)
    ang = jnp.arange(seq, dtype=F32)[:, None] * inv[None, :]
    cos, sin = jnp.cos(ang), jnp.sin(ang)
    z = lambda n: jnp.zeros((seq, n), F32)
    tail = LANES - MLA_QK
    rc = jnp.concatenate([jnp.ones((seq, MLA_NOPE), F32), cos, cos, z(tail)], axis=1)
    rs1 = jnp.concatenate([z(MLA_NOPE), -sin, z(half), z(tail)], axis=1)
    rs2 = jnp.concatenate([z(MLA_NOPE), z(half), sin, z(tail)], axis=1)
    return rc, rs1, rs2


SWA_HEAD_ORDER = tuple(h for m in range(SWA_GROUP) for h in (m, m + SWA_GROUP))


def _layer0_weights(ev_w_in, ev_w_out, mla_w_uq, mla_w_ukv, mla_g_q, mla_g_k, swa_g_q, swa_g_k):
    pad_head = LANES - MLA_QK
    mla_in = MLA_Q_RANK + MLA_KV_RANK + MLA_ROPE
    nq = SWA_HEADS * SWA_HEAD_DIM
    nkv = SWA_KV_HEADS * SWA_HEAD_DIM
    order = jnp.asarray(SWA_HEAD_ORDER)
    w_cq_ckv = ev_w_in[:, :MLA_Q_RANK + MLA_KV_RANK]
    w_kpe = ev_w_in[:, MLA_Q_RANK + MLA_KV_RANK:mla_in]
    w_sq = ev_w_in[:, mla_in:mla_in + nq].reshape(D_MODEL, SWA_HEADS, SWA_HEAD_DIM)[:, order].reshape(D_MODEL, nq)
    w_skv = ev_w_in[:, mla_in + nq:mla_in + nq + 2 * nkv]
    win = jnp.concatenate([w_cq_ckv, w_sq, w_skv, w_kpe, jnp.zeros((D_MODEL, LANES - MLA_ROPE), F32)], axis=1)
    wq = jnp.pad(mla_w_uq, ((0, 0), (0, 0), (0, pad_head))).reshape(MLA_Q_RANK, MLA_HEADS * LANES)
    wkc = jnp.pad(mla_w_ukv[:, :, :MLA_NOPE], ((0, 0), (0, 0), (0, LANES - MLA_NOPE))).reshape(MLA_KV_RANK, MLA_HEADS * LANES)
    place = jnp.pad(jnp.eye(MLA_ROPE, dtype=F32), ((0, LANES - MLA_ROPE), (MLA_NOPE, pad_head)))
    wkp = jnp.tile(place, (1, MLA_HEADS))
    wv = mla_w_ukv[:, :, MLA_NOPE:].reshape(MLA_KV_RANK, MLA_HEADS * MLA_V)
    gq = jnp.pad(mla_g_q, (0, pad_head)).reshape(1, LANES)
    gk = jnp.pad(mla_g_k, (0, pad_head)).reshape(1, LANES)
    sgq = jnp.tile(swa_g_q, 2).reshape(1, LANES)
    sgk = jnp.tile(swa_g_k, 2).reshape(1, LANES)
    n_a = MLA_HEADS * MLA_V
    w_out_b = ev_w_out[n_a:].reshape(SWA_HEADS, SWA_HEAD_DIM, D_MODEL)[order].reshape(nq, D_MODEL)
    wout = jnp.concatenate([ev_w_out[:n_a], w_out_b], axis=0)
    return dict(win=win.astype(BF16), wq=wq.astype(BF16), wkc=wkc.astype(BF16), wkp=wkp.astype(BF16),
                wv=wv.astype(BF16), gq=gq, gk=gk, sgq=sgq, sgk=sgk, wout=wout.astype(BF16))


def _peer_layer(x2d, xnt, w_query, sub_keys, u_emb, v_emb):
    wqt = w_query.reshape(D_MODEL, 2 * PEER_HEADS * PEER_HALF).T.astype(BF16)
    sk = sub_keys.reshape(2 * PEER_HEADS, PEER_N_KEYS, PEER_HALF).astype(BF16)
    return _peer_call(x2d, xnt, wqt, sk, u_emb.astype(BF16), v_emb.T.astype(BF16))


def kernel(x, t5_bias, norm_mix, norm_ffn, ev_w_in, ev_w_out, mla_g_cq, mla_w_uq, mla_g_ckv, mla_w_ukv,
           mla_g_q, mla_g_k, swa_g_q, swa_g_k, swa_sinks, od_w_in, od_w_out, ca_g_q, ca_g_k, ca_rel_bias,
           peer_w_query, peer_sub_keys, peer_u, peer_v):
    B, S, D = x.shape
    T = B * S
    x2d = x.reshape(T, D)
    row = lambda a: a.reshape(1, -1)

    w0 = _layer0_weights(ev_w_in[0], ev_w_out[0], mla_w_uq[0], mla_w_ukv[0], mla_g_q[0], mla_g_k[0],
                         swa_g_q[0], swa_g_k[0])
    rc, rs1, rs2 = _rope_tables(S)
    qm, km, vm, qs, ks, vs = _in0_call(
        x2d, S, row(norm_mix[0]), w0["win"], row(mla_g_cq[0]), w0["wq"], w0["gq"], row(mla_g_ckv[0]),
        w0["wkc"], w0["wkp"], w0["gk"], w0["wv"], w0["sgq"], w0["sgk"], rc, rs1, rs2)
    b3 = lambda a: a.reshape(B, S, a.shape[1])
    o_a = _mla_call(b3(qm), b3(km), b3(vm))
    o_b = _swa_call(swa_sinks[0], b3(qs), b3(ks), b3(vs), _swa_bias(t5_bias))
    o0 = jnp.concatenate([o_a, o_b], axis=-1).reshape(T, -1)
    x2d, xnt = _out_call(x2d, o0, w0["wout"], row(norm_ffn[0]))
    x2d = _peer_layer(x2d, xnt, peer_w_query[0], peer_sub_keys[0], peer_u[0], peer_v[0])

    gq2 = jnp.tile(ca_g_q[0], 2).reshape(1, LANES)
    gk2 = jnp.tile(ca_g_k[0], 2).reshape(1, LANES)
    qc, kc, vc = _in1_call(x2d, row(norm_mix[1]), od_w_in[0].astype(BF16), gq2, gk2)
    o_c = _ca_call(b3(qc), b3(kc), b3(vc), _ca_bias(ca_rel_bias[0]))
    x2d, xnt = _out_call(x2d, o_c.reshape(T, -1), od_w_out[0].astype(BF16), row(norm_ffn[1]))
    x2d = _peer_layer(x2d, xnt, peer_w_query[1], peer_sub_keys[1], peer_u[1], peer_v[1])
    return x2d.reshape(B, S, D)
```

```python
import functools
import math

import jax
import jax.numpy as jnp
import numpy as np
from jax import lax
from jax.experimental import pallas as pl
from jax.experimental.pallas import tpu as pltpu

F32 = jnp.float32
BF16 = jnp.bfloat16

D_MODEL = 1024
CHUNK = 64
EPS = 1e-6
NEG = -1e30
LANES = 128
HALF_LANES = LANES // 2

T5_BUCKETS = 32
T5_MAX_DIST = 128

MLA_HEADS = 8
MLA_NOPE = 64
MLA_ROPE = 32
MLA_V = 64
MLA_QK = MLA_NOPE + MLA_ROPE
MLA_Q_RANK = 384
MLA_KV_RANK = 256
ROPE_THETA = 10000.0

SWA_HEADS = 8
SWA_KV_HEADS = 2
SWA_GROUP = SWA_HEADS // SWA_KV_HEADS
SWA_HEAD_DIM = 64
SWA_BAND = 2

CA_HEADS = 16
CA_HEAD_DIM = 64
CA_LEFT_CHUNKS = 8
CA_REL_PAST = 256

PEER_HEADS = 8
PEER_N_KEYS = 128
PEER_HALF = 128
PEER_TOPK = 16

VMEM_LIMIT = 56 * 1024 * 1024

TM_PROJ = 256
TQ_MLA = 512
TQ_BAND = 128
TQ_CA = 256
TQ_PEER = 512
EB_PEER = 1024
TS_PEER = 256
QR_PEER = 16

GELU_C0 = math.sqrt(2.0 / math.pi)
GELU_C1 = GELU_C0 * 0.044715


def _cparams(sem):
    return pltpu.CompilerParams(dimension_semantics=sem, vmem_limit_bytes=VMEM_LIMIT)


def _rms(x, g):
    ms = jnp.mean(x * x, axis=-1, keepdims=True)
    return x * lax.rsqrt(ms + EPS) * g


def _rms_two_heads(x, g2, scale):
    lane = lax.broadcasted_iota(jnp.int32, x.shape, 1)
    lo_mask = lane < HALF_LANES
    s = x * x
    lo = jnp.sum(jnp.where(lo_mask, s, 0.0), axis=-1, keepdims=True)
    hi = jnp.sum(jnp.where(lo_mask, 0.0, s), axis=-1, keepdims=True)
    r = jnp.where(lo_mask, lax.rsqrt(lo * (1.0 / HALF_LANES) + EPS), lax.rsqrt(hi * (1.0 / HALF_LANES) + EPS))
    return x * r * (g2 * scale)


def _dot_nt(a, b):
    return lax.dot_general(a, b, (((1,), (1,)), ((), ())), preferred_element_type=F32)


def _in0_kernel(x_ref, gmix_ref, win_ref, gcq_ref, wq_ref, gq_ref, gckv_ref, wkc_ref, wkp_ref, gk_ref,
                wv_ref, sgq_ref, sgk_ref, rc_ref, rs1_ref, rs2_ref,
                qm_ref, km_ref, vm_ref, qs_ref, ks_ref, vs_ref):
    xn = _rms(x_ref[...], gmix_ref[...]).astype(BF16)
    p = jnp.dot(xn, win_ref[...], preferred_element_type=F32)
    cq = _rms(p[:, 0:MLA_Q_RANK], gcq_ref[...]).astype(BF16)
    ckv = _rms(p[:, 384:640], gckv_ref[...]).astype(BF16)
    kpe = p[:, 1408:1536].astype(BF16)
    q = jnp.dot(cq, wq_ref[...], preferred_element_type=F32)
    k = (jnp.dot(ckv, wkc_ref[...], preferred_element_type=F32)
         + jnp.dot(kpe, wkp_ref[...], preferred_element_type=F32))
    vm_ref[...] = jnp.dot(ckv, wv_ref[...], preferred_element_type=F32).astype(BF16)

    rc, rs1, rs2 = rc_ref[...], rs1_ref[...], rs2_ref[...]

    def head_norm_rope(t, g, scale):
        ms = jnp.sum(t * t, axis=-1, keepdims=True) * (1.0 / MLA_QK)
        tn = t * lax.rsqrt(ms + EPS) * g
        tr = tn * rc + pltpu.roll(tn, LANES - MLA_ROPE // 2, 1) * rs1 + pltpu.roll(tn, MLA_ROPE // 2, 1) * rs2
        return (tr * scale).astype(BF16)

    for h in range(MLA_HEADS):
        sl = slice(h * LANES, (h + 1) * LANES)
        qm_ref[:, sl] = head_norm_rope(q[:, sl], gq_ref[...], MLA_QK ** -0.5)
        km_ref[:, sl] = head_norm_rope(k[:, sl], gk_ref[...], 1.0)

    for m in range(SWA_HEADS // 2):
        sl = slice(640 + m * LANES, 640 + (m + 1) * LANES)
        qs_ref[:, m * LANES:(m + 1) * LANES] = _rms_two_heads(p[:, sl], sgq_ref[...], SWA_HEAD_DIM ** -0.5).astype(BF16)
    ks_ref[...] = _rms_two_heads(p[:, 1152:1280], sgk_ref[...], 1.0).astype(BF16)
    vs_ref[...] = p[:, 1280:1408].astype(BF16)


def _in0_call(x2d, seq, gmix, win, gcq, wq, gq, gckv, wkc, wkp, gk, wv, sgq, sgk, rc, rs1, rs2):
    T = x2d.shape[0]
    tm = TM_PROJ
    nblk_seq = seq // tm
    row = lambda i: (i, 0)
    const = lambda i: (0, 0)
    pos = lambda i: (i % nblk_seq, 0)
    full = lambda a: pl.BlockSpec(a.shape, const)
    out_shapes = (
        jax.ShapeDtypeStruct((T, MLA_HEADS * LANES), BF16),
        jax.ShapeDtypeStruct((T, MLA_HEADS * LANES), BF16),
        jax.ShapeDtypeStruct((T, MLA_HEADS * MLA_V), BF16),
        jax.ShapeDtypeStruct((T, SWA_HEADS * SWA_HEAD_DIM), BF16),
        jax.ShapeDtypeStruct((T, SWA_KV_HEADS * SWA_HEAD_DIM), BF16),
        jax.ShapeDtypeStruct((T, SWA_KV_HEADS * SWA_HEAD_DIM), BF16),
    )
    return pl.pallas_call(
        _in0_kernel,
        grid=(T // tm,),
        in_specs=[pl.BlockSpec((tm, D_MODEL), row), full(gmix), full(win), full(gcq), full(wq), full(gq),
                  full(gckv), full(wkc), full(wkp), full(gk), full(wv), full(sgq), full(sgk),
                  pl.BlockSpec((tm, LANES), pos), pl.BlockSpec((tm, LANES), pos), pl.BlockSpec((tm, LANES), pos)],
        out_specs=tuple(pl.BlockSpec((tm, s.shape[1]), row) for s in out_shapes),
        out_shape=out_shapes,
        compiler_params=_cparams(("parallel",)),
        name="layer0_in_proj",
    )(x2d, gmix, win, gcq, wq, gq, gckv, wkc, wkp, gk, wv, sgq, sgk, rc, rs1, rs2)


def _mla_kernel(q_ref, k_ref, v_ref, o_ref, m_scr, l_scr, acc_scr):
    i = pl.program_id(1)
    tq = TQ_MLA
    rows = lax.broadcasted_iota(jnp.int32, (tq, tq), 0)
    cols = lax.broadcasted_iota(jnp.int32, (tq, tq), 1)
    diag_ok = (cols // CHUNK) <= (rows // CHUNK)
    lane = lax.broadcasted_iota(jnp.int32, (tq, LANES), 1)
    m_scr[...] = jnp.full(m_scr.shape, NEG, F32)
    l_scr[...] = jnp.zeros_like(l_scr)
    acc_scr[...] = jnp.zeros_like(acc_scr)

    def body(j, _):
        start = pl.multiple_of(j * tq, tq)
        ok = jnp.logical_or(j < i, diag_ok)

        def scores(h):
            return _dot_nt(q_ref[0, :, h * LANES:(h + 1) * LANES],
                           k_ref[0, pl.ds(start, tq), h * LANES:(h + 1) * LANES])

        s_next = scores(0)
        for h in range(MLA_HEADS):
            s = jnp.where(ok, s_next, NEG)
            if h + 1 < MLA_HEADS:
                s_next = scores(h + 1)
            vb = v_ref[0, pl.ds(start, tq), (h // 2) * LANES:(h // 2 + 1) * LANES]
            m_old = m_scr[h]
            m_new = jnp.maximum(m_old, jnp.max(s, axis=-1, keepdims=True))
            alpha = jnp.exp(m_old - m_new)
            p = jnp.exp(s - m_new)
            l_scr[h] = alpha * l_scr[h] + jnp.sum(p, axis=-1, keepdims=True)
            acc_scr[h] = alpha * acc_scr[h] + jnp.dot(p.astype(BF16), vb, preferred_element_type=F32)
            m_scr[h] = m_new
        return 0

    lax.fori_loop(0, i + 1, body, 0)
    for hp in range(MLA_HEADS // 2):
        lo = acc_scr[2 * hp] / l_scr[2 * hp]
        hi = acc_scr[2 * hp + 1] / l_scr[2 * hp + 1]
        o_ref[0, :, hp * LANES:(hp + 1) * LANES] = jnp.where(lane < HALF_LANES, lo, hi).astype(BF16)


def _mla_call(q, k, v):
    B, S, _ = q.shape
    tq = TQ_MLA
    return pl.pallas_call(
        _mla_kernel,
        grid=(B, S // tq),
        in_specs=[pl.BlockSpec((1, tq, q.shape[2]), lambda b, i: (b, i, 0)),
                  pl.BlockSpec((1, S, k.shape[2]), lambda b, i: (b, 0, 0)),
                  pl.BlockSpec((1, S, v.shape[2]), lambda b, i: (b, 0, 0))],
        out_specs=pl.BlockSpec((1, tq, v.shape[2]), lambda b, i: (b, i, 0)),
        out_shape=jax.ShapeDtypeStruct((B, S, v.shape[2]), BF16),
        scratch_shapes=[pltpu.VMEM((MLA_HEADS, tq, 1), F32), pltpu.VMEM((MLA_HEADS, tq, 1), F32),
                        pltpu.VMEM((MLA_HEADS, tq, LANES), F32)],
        compiler_params=_cparams(("parallel", "arbitrary")),
        name="latent_attention",
    )(q, k, v)


def _swa_kernel(sink_ref, q_ref, kp_ref, kc_ref, vp_ref, vc_ref, bias_ref, o_ref):
    i = pl.program_id(1)
    tq = TQ_BAND
    kk = jnp.concatenate([kp_ref[0], kc_ref[0]], axis=0)
    vv = jnp.concatenate([vp_ref[0], vc_ref[0]], axis=0)
    lane = lax.broadcasted_iota(jnp.int32, (tq, LANES), 1)
    col = lax.broadcasted_iota(jnp.int32, (tq, 2 * tq), 1)
    key_ok = jnp.logical_or(col >= tq, i > 0)
    for m in range(SWA_HEADS // 2):
        qq = q_ref[0, :, m * LANES:(m + 1) * LANES]
        zero = jnp.zeros_like(qq)
        outs = []
        for hh in range(2):
            h = m + hh * SWA_GROUP
            qx = jnp.where(lane < HALF_LANES, qq, zero) if hh == 0 else jnp.where(lane < HALF_LANES, zero, qq)
            s = _dot_nt(qx, kk) + bias_ref[h]
            s = jnp.where(key_ok, s, NEG)
            sink = sink_ref[h]
            mx = jnp.maximum(jnp.max(s, axis=-1, keepdims=True), sink)
            p = jnp.exp(s - mx)
            den = jnp.sum(p, axis=-1, keepdims=True) + jnp.exp(sink - mx)
            o = jnp.dot(p.astype(BF16), vv, preferred_element_type=F32)
            outs.append(o / den)
        o_ref[0, :, m * LANES:(m + 1) * LANES] = jnp.where(lane < HALF_LANES, outs[0], outs[1]).astype(BF16)


def _swa_call(sinks, q, k, v, bias):
    B, S, _ = q.shape
    tq = TQ_BAND
    prev = lambda b, i: (b, jnp.maximum(i - 1, 0), 0)
    cur = lambda b, i: (b, i, 0)
    return pl.pallas_call(
        _swa_kernel,
        grid=(B, S // tq),
        in_specs=[pl.BlockSpec(memory_space=pltpu.SMEM),
                  pl.BlockSpec((1, tq, q.shape[2]), cur),
                  pl.BlockSpec((1, tq, LANES), prev), pl.BlockSpec((1, tq, LANES), cur),
                  pl.BlockSpec((1, tq, LANES), prev), pl.BlockSpec((1, tq, LANES), cur),
                  pl.BlockSpec(bias.shape, lambda b, i: (0, 0, 0))],
        out_specs=pl.BlockSpec((1, tq, q.shape[2]), cur),
        out_shape=jax.ShapeDtypeStruct(q.shape, BF16),
        compiler_params=_cparams(("parallel", "arbitrary")),
        name="sliding_window_attention",
    )(sinks, q, k, k, v, v, bias)


CA_KBLOCKS = CA_LEFT_CHUNKS * CHUNK // TQ_CA + 1


def _ca_kernel(q_ref, k_ref, v_ref, bias_ref, o_ref):
    i = pl.program_id(1)
    tq = TQ_CA
    nk = CA_KBLOCKS * tq
    lane = lax.broadcasted_iota(jnp.int32, (tq, LANES), 1)
    col = lax.broadcasted_iota(jnp.int32, (tq, nk), 1)
    key_ok = (col // tq) >= (CA_KBLOCKS - 1 - i)
    starts = [pl.multiple_of(jnp.maximum(i - (CA_KBLOCKS - 1) + j, 0) * tq, tq) for j in range(CA_KBLOCKS)]

    def scores(h):
        sl = slice((h // 2) * LANES, (h // 2 + 1) * LANES)
        kk = jnp.concatenate([k_ref[0, pl.ds(st, tq), sl] for st in starts], axis=0)
        qq = q_ref[0, :, sl]
        zero = jnp.zeros_like(qq)
        qx = jnp.where(lane < HALF_LANES, qq, zero) if h % 2 == 0 else jnp.where(lane < HALF_LANES, zero, qq)
        return _dot_nt(qx, kk)

    s_next = scores(0)
    outs = []
    for h in range(CA_HEADS):
        sl = slice((h // 2) * LANES, (h // 2 + 1) * LANES)
        s = jnp.where(key_ok, s_next + bias_ref[h], NEG)
        if h + 1 < CA_HEADS:
            s_next = scores(h + 1)
        vv = jnp.concatenate([v_ref[0, pl.ds(st, tq), sl] for st in starts], axis=0)
        mx = jnp.max(s, axis=-1, keepdims=True)
        p = jnp.exp(s - mx)
        den = jnp.sum(p, axis=-1, keepdims=True)
        o = jnp.dot(p.astype(BF16), vv, preferred_element_type=F32)
        outs.append(o / den)
        if h % 2 == 1:
            o_ref[0, :, sl] = jnp.where(lane < HALF_LANES, outs[0], outs[1]).astype(BF16)
            outs = []


def _ca_call(q, k, v, bias):
    B, S, W = q.shape
    tq = TQ_CA
    return pl.pallas_call(
        _ca_kernel,
        grid=(B, S // tq),
        in_specs=[pl.BlockSpec((1, tq, W), lambda b, i: (b, i, 0)),
                  pl.BlockSpec((1, S, W), lambda b, i: (b, 0, 0)),
                  pl.BlockSpec((1, S, W), lambda b, i: (b, 0, 0)),
                  pl.BlockSpec(bias.shape, lambda b, i: (0, 0, 0))],
        out_specs=pl.BlockSpec((1, tq, W), lambda b, i: (b, i, 0)),
        out_shape=jax.ShapeDtypeStruct(q.shape, BF16),
        compiler_params=_cparams(("parallel", "arbitrary")),
        name="chunk_attention",
    )(q, k, v, bias)


def _in1_kernel(x_ref, gmix_ref, win_ref, gq_ref, gk_ref, q_ref, k_ref, v_ref):
    xn = _rms(x_ref[...], gmix_ref[...]).astype(BF16)
    p = jnp.dot(xn, win_ref[...], preferred_element_type=F32)
    w = CA_HEADS * CA_HEAD_DIM
    for m in range(CA_HEADS // 2):
        sl = slice(m * LANES, (m + 1) * LANES)
        q_ref[:, sl] = _rms_two_heads(p[:, m * LANES:(m + 1) * LANES], gq_ref[...], CA_HEAD_DIM ** -0.5).astype(BF16)
        k_ref[:, sl] = _rms_two_heads(p[:, w + m * LANES:w + (m + 1) * LANES], gk_ref[...], 1.0).astype(BF16)
    v_ref[...] = p[:, 2 * w:3 * w].astype(BF16)


def _in1_call(x2d, gmix, win, gq, gk):
    T = x2d.shape[0]
    tm = TM_PROJ
    w = CA_HEADS * CA_HEAD_DIM
    row = lambda i: (i, 0)
    const = lambda i: (0, 0)
    full = lambda a: pl.BlockSpec(a.shape, const)
    out_shapes = tuple(jax.ShapeDtypeStruct((T, w), BF16) for _ in range(3))
    return pl.pallas_call(
        _in1_kernel,
        grid=(T // tm,),
        in_specs=[pl.BlockSpec((tm, D_MODEL), row), full(gmix), full(win), full(gq), full(gk)],
        out_specs=tuple(pl.BlockSpec((tm, w), row) for _ in range(3)),
        out_shape=out_shapes,
        compiler_params=_cparams(("parallel",)),
        name="layer1_in_proj",
    )(x2d, gmix, win, gq, gk)


def _out_kernel(x_ref, o_ref, w_ref, g_ref, xo_ref, xnt_ref):
    y = x_ref[...] + jnp.dot(o_ref[...], w_ref[...], preferred_element_type=F32)
    xo_ref[...] = y
    xnt_ref[...] = _rms(y, g_ref[...]).T.astype(BF16)


def _out_call(x2d, o2d, w, g):
    T = x2d.shape[0]
    tm = TM_PROJ
    row = lambda i: (i, 0)
    const = lambda i: (0, 0)
    return pl.pallas_call(
        _out_kernel,
        grid=(T // tm,),
        in_specs=[pl.BlockSpec((tm, D_MODEL), row), pl.BlockSpec((tm, o2d.shape[1]), row),
                  pl.BlockSpec(w.shape, const), pl.BlockSpec(g.shape, const)],
        out_specs=(pl.BlockSpec((tm, D_MODEL), row), pl.BlockSpec((D_MODEL, tm), lambda i: (0, i))),
        out_shape=(jax.ShapeDtypeStruct((T, D_MODEL), F32), jax.ShapeDtypeStruct((D_MODEL, T), BF16)),
        compiler_params=_cparams(("parallel",)),
        name="out_proj_residual",
    )(x2d, o2d, w, g)


NTB_PEER = TQ_PEER // LANES
I1_PER_STEP = EB_PEER // PEER_N_KEYS


SUBLANES = 8
BIG = 3.0e38

SORT16 = (
    (0, 13), (1, 12), (2, 15), (3, 14), (4, 8), (5, 6), (7, 11), (9, 10),
    (0, 5), (1, 7), (2, 9), (3, 4), (6, 13), (8, 14), (10, 15), (11, 12),
    (0, 1), (2, 3), (4, 5), (6, 8), (7, 9), (10, 11), (12, 13), (14, 15),
    (0, 2), (1, 3), (4, 10), (5, 11), (6, 7), (8, 9), (12, 14), (13, 15),
    (1, 2), (3, 12), (4, 6), (5, 7), (8, 10), (9, 11), (13, 14),
    (1, 4), (2, 6), (5, 8), (7, 10), (9, 13), (11, 14),
    (2, 4), (3, 6), (9, 12), (11, 13),
    (3, 5), (6, 8), (7, 9), (10, 12),
    (3, 4), (5, 6), (7, 8), (9, 10), (11, 12),
    (6, 7), (8, 9),
)
BITONIC16 = tuple((r, r + st) for st in (8, 4, 2, 1) for r in range(16) if (r // st) % 2 == 0)


def _compare_exchange(v, i, j):
    hi, lo = jnp.maximum(v[i], v[j]), jnp.minimum(v[i], v[j])
    v[i], v[j] = hi, lo


def _merge_across_sublanes(v):
    for shift in (4, 2, 1):
        y = [pltpu.roll(t, shift, 0) for t in v]
        v = [jnp.maximum(v[r], y[PEER_TOPK - 1 - r]) for r in range(PEER_TOPK)]
        for i, j in BITONIC16:
            _compare_exchange(v, i, j)
    return v


def _top16(x):
    v = [x[SUBLANES * i:SUBLANES * (i + 1), :] for i in range(PEER_N_KEYS // SUBLANES)]
    for i, j in SORT16:
        _compare_exchange(v, i, j)
    return _merge_across_sublanes(v)


def _route_tile(s1, s2):
    ta = _top16(s1)
    tb = _top16(s2)
    sub = lax.broadcasted_iota(jnp.int32, (SUBLANES, LANES), 0)
    a8 = ta[SUBLANES - 1]
    for j in range(SUBLANES - 2, -1, -1):
        a8 = jnp.where(sub == j, ta[j], a8)
    cl = []
    for k in range(PEER_TOPK):
        nj = PEER_TOPK // (k + 1)
        l = a8 + tb[k]
        cl.append(l if nj >= SUBLANES else jnp.where(sub < nj, l, NEG))
    m = _merge_across_sublanes(list(cl))
    ex = [ta[SUBLANES + r] + tb[0] for r in range(SUBLANES)]
    top = m[:SUBLANES] + [jnp.maximum(m[r], ex[PEER_TOPK - 1 - r]) for r in range(SUBLANES, PEER_TOPK)]
    for i, j in BITONIC16:
        _compare_exchange(top, i, j)
    tau = top[PEER_TOPK - 1]
    z = jnp.exp(top[0] - top[0])
    for r in range(1, PEER_TOPK):
        z = z + jnp.exp(top[r] - top[0])
    rz = 0.5 / z

    e2k = [jnp.exp(tb[k] - tb[0]) for k in range(PEER_TOPK)]
    e1top = jnp.exp(a8 - ta[0]) * rz
    thr = jnp.full((SUBLANES, LANES), BIG, F32)
    for k in range(PEER_TOPK):
        thr = jnp.minimum(thr, jnp.where(cl[k] >= tau, e1top * e2k[k], BIG))
    for r in range(SUBLANES):
        e1x = jnp.exp(ta[SUBLANES + r] - ta[0]) * rz
        thr = jnp.minimum(thr, jnp.where(ex[r] >= tau, e1x * e2k[0], BIG))
    for shift in (4, 2, 1):
        thr = jnp.minimum(thr, pltpu.roll(thr, shift, 0))
    n = PEER_N_KEYS // SUBLANES
    e1 = jnp.concatenate([jnp.exp(s1[SUBLANES * i:SUBLANES * (i + 1), :] - ta[0]) * rz for i in range(n)], axis=0)
    e2 = jnp.concatenate([jnp.exp(s2[SUBLANES * i:SUBLANES * (i + 1), :] - tb[0]) for i in range(n)], axis=0)
    return e1, e2, thr


def _peer_kernel(x_ref, xnt_ref, wqt_ref, sk_ref, u_ref, vt_ref, o_ref,
                 q_scr, sc_scr, e1_scr, e2_scr, thr_scr, a_scr, g_scr, acc_scr):
    e = pl.program_id(1)

    @pl.when(e == 0)
    def _route():
        q_scr[...] = jnp.dot(wqt_ref[...], xnt_ref[...], preferred_element_type=F32).astype(BF16)
        acc_scr[...] = jnp.zeros_like(acc_scr)

        def per_head(h, _):
            for p in range(2):
                hp = h * 2 + p
                qs = q_scr[pl.ds(pl.multiple_of(hp * PEER_HALF, PEER_HALF), PEER_HALF), :]
                sc = jnp.dot(sk_ref[hp], qs, preferred_element_type=F32)
                for tb in range(NTB_PEER):
                    sc_scr[p, tb] = sc[:, tb * LANES:(tb + 1) * LANES]

            def per_tile(tb, _):
                e1, e2, thr = _route_tile(sc_scr[0, tb], sc_scr[1, tb])
                e1_scr[h, tb] = e1
                e2_scr[h, tb] = e2
                thr_scr[h, tb] = thr
                return 0

            lax.fori_loop(0, NTB_PEER, per_tile, 0)
            return 0

        lax.fori_loop(0, PEER_HEADS, per_head, 0)

    n_sub = TQ_PEER // TS_PEER

    def route_weights(s):
        for tb2 in range(TS_PEER // LANES):
            tb = s * (TS_PEER // LANES) + tb2
            ts = slice(tb2 * LANES, (tb2 + 1) * LANES)
            for q in range(PEER_N_KEYS // QR_PEER):
                wsum = [jnp.zeros((QR_PEER, LANES), F32) for _ in range(I1_PER_STEP)]
                for h in range(PEER_HEADS):
                    e2q = jnp.maximum(e2_scr[h, tb, q * QR_PEER:(q + 1) * QR_PEER, :], 0.0)
                    thr = thr_scr[h, tb, 0:1, :]
                    for r in range(I1_PER_STEP):
                        w = e2q * e1_scr[h, tb, pl.ds(e * I1_PER_STEP + r, 1), :]
                        wsum[r] = wsum[r] + jnp.where(w >= thr, w, 0.0)
                for r in range(I1_PER_STEP):
                    rows = slice(r * PEER_N_KEYS + q * QR_PEER, r * PEER_N_KEYS + (q + 1) * QR_PEER)
                    at = a_scr[s, rows, ts]
                    g = wsum[r] * at * (1.0 + jnp.tanh(at * (GELU_C0 + GELU_C1 * (at * at))))
                    g_scr[s, rows, ts] = g.astype(BF16)

    for s in range(n_sub):
        a_scr[s] = jnp.dot(u_ref[...], xnt_ref[:, s * TS_PEER:(s + 1) * TS_PEER],
                           preferred_element_type=F32)
    for s in range(n_sub):
        route_weights(s)
        acc_scr[:, s * TS_PEER:(s + 1) * TS_PEER] += jnp.dot(vt_ref[...], g_scr[s], preferred_element_type=F32)

    @pl.when(e == pl.num_programs(1) - 1)
    def _finish():
        o_ref[...] = x_ref[...] + acc_scr[...].T


def _peer_call(x2d, xnt, wqt, sk, u, vt):
    T = x2d.shape[0]
    tq, eb = TQ_PEER, EB_PEER
    n_exp = u.shape[0]
    return pl.pallas_call(
        _peer_kernel,
        grid=(T // tq, n_exp // eb),
        in_specs=[pl.BlockSpec((tq, D_MODEL), lambda t, e: (t, 0)),
                  pl.BlockSpec((D_MODEL, tq), lambda t, e: (0, t)),
                  pl.BlockSpec(wqt.shape, lambda t, e: (0, 0)),
                  pl.BlockSpec(sk.shape, lambda t, e: (0, 0, 0)),
                  pl.BlockSpec((eb, D_MODEL), lambda t, e: (e, 0)),
                  pl.BlockSpec((D_MODEL, eb), lambda t, e: (0, e))],
        out_specs=pl.BlockSpec((tq, D_MODEL), lambda t, e: (t, 0)),
        out_shape=jax.ShapeDtypeStruct((T, D_MODEL), F32),
        scratch_shapes=[
            pltpu.VMEM((2 * PEER_HEADS * PEER_HALF, tq), BF16),
            pltpu.VMEM((2, NTB_PEER, PEER_N_KEYS, LANES), F32),
            pltpu.VMEM((PEER_HEADS, NTB_PEER, PEER_N_KEYS, LANES), F32),
            pltpu.VMEM((PEER_HEADS, NTB_PEER, PEER_N_KEYS, LANES), F32),
            pltpu.VMEM((PEER_HEADS, NTB_PEER, SUBLANES, LANES), F32),
            pltpu.VMEM((tq // TS_PEER, eb, TS_PEER), F32),
            pltpu.VMEM((tq // TS_PEER, eb, TS_PEER), BF16),
            pltpu.VMEM((D_MODEL, tq), F32),
        ],
        compiler_params=_cparams(("parallel", "arbitrary")),
        name="peer_dense",
    )(x2d, xnt, wqt, sk, u, vt)


def _t5_bucket(rel):
    nb = T5_BUCKETS // 2
    max_exact = nb // 2
    ret = jnp.where(rel > 0, nb, 0)
    n = jnp.abs(rel)
    large = max_exact + (jnp.log(jnp.maximum(n, 1).astype(F32) / max_exact)
                         / math.log(T5_MAX_DIST / max_exact) * (nb - max_exact)).astype(jnp.int32)
    large = jnp.minimum(large, nb - 1)
    return ret + jnp.where(n < max_exact, n, large)


def _toeplitz(f, n_rows, n_cols):
    return jnp.stack([f[:, n_rows - 1 - r:n_rows - 1 - r + n_cols] for r in range(n_rows)], axis=1)


def _swa_bias(t5_table):
    tq = TQ_BAND
    r = jnp.arange(tq)[:, None]
    c = jnp.arange(2 * tq)[None, :]
    rel_line = jnp.arange(3 * tq - 1) - (2 * tq - 1)
    bias = _toeplitz(t5_table[_t5_bucket(rel_line)].astype(F32).T, tq, 2 * tq)
    qc = (tq + r) // CHUNK
    kc = c // CHUNK
    vis = jnp.logical_and(kc >= qc - SWA_BAND, kc <= qc)
    return jnp.where(vis[None], bias, NEG)


def _ca_bias(rel_table):
    tq = TQ_CA
    left = CA_LEFT_CHUNKS * CHUNK
    r = jnp.arange(tq)[:, None]
    c = jnp.arange(left + tq)[None, :]
    rel_line = jnp.arange(left + 2 * tq - 1) - (left + tq - 1)
    idx = jnp.clip(rel_line, -CA_REL_PAST, CHUNK - 1) + CA_REL_PAST
    bias = _toeplitz(rel_table[:, idx].astype(F32), tq, left + tq)
    qc = (left + r) // CHUNK
    kc = c // CHUNK
    vis = jnp.logical_and(kc >= qc - CA_LEFT_CHUNKS, kc <= qc)
    return jnp.where(vis[None], bias, NEG)


def _rope_tables(seq):
    half = MLA_ROPE // 2
    inv = ROPE_THETA ** (-jnp.arange(0, MLA_ROPE, 2, dtype=F32) / MLA_ROPE)
    ang = jnp.arange(seq, dtype=F32)[:, None] * inv[None, :]
    cos, sin = jnp.cos(ang), jnp.sin(ang)
    z = lambda n: jnp.zeros((seq, n), F32)
    tail = LANES - MLA_QK
    rc = jnp.concatenate([jnp.ones((seq, MLA_NOPE), F32), cos, cos, z(tail)], axis=1)
    rs1 = jnp.concatenate([z(MLA_NOPE), -sin, z(half), z(tail)], axis=1)
    rs2 = jnp.concatenate([z(MLA_NOPE), z(half), sin, z(tail)], axis=1)
    return rc, rs1, rs2


SWA_HEAD_ORDER = tuple(h for m in range(SWA_GROUP) for h in (m, m + SWA_GROUP))


def _layer0_weights(ev_w_in, ev_w_out, mla_w_uq, mla_w_ukv, mla_g_q, mla_g_k, swa_g_q, swa_g_k):
    pad_head = LANES - MLA_QK
    mla_in = MLA_Q_RANK + MLA_KV_RANK + MLA_ROPE
    nq = SWA_HEADS * SWA_HEAD_DIM
    nkv = SWA_KV_HEADS * SWA_HEAD_DIM
    order = jnp.asarray(SWA_HEAD_ORDER)
    w_cq_ckv = ev_w_in[:, :MLA_Q_RANK + MLA_KV_RANK]
    w_kpe = ev_w_in[:, MLA_Q_RANK + MLA_KV_RANK:mla_in]
    w_sq = ev_w_in[:, mla_in:mla_in + nq].reshape(D_MODEL, SWA_HEADS, SWA_HEAD_DIM)[:, order].reshape(D_MODEL, nq)
    w_skv = ev_w_in[:, mla_in + nq:mla_in + nq + 2 * nkv]
    win = jnp.concatenate([w_cq_ckv, w_sq, w_skv, w_kpe, jnp.zeros((D_MODEL, LANES - MLA_ROPE), F32)], axis=1)
    wq = jnp.pad(mla_w_uq, ((0, 0), (0, 0), (0, pad_head))).reshape(MLA_Q_RANK, MLA_HEADS * LANES)
    wkc = jnp.pad(mla_w_ukv[:, :, :MLA_NOPE], ((0, 0), (0, 0), (0, LANES - MLA_NOPE))).reshape(MLA_KV_RANK, MLA_HEADS * LANES)
    place = jnp.pad(jnp.eye(MLA_ROPE, dtype=F32), ((0, LANES - MLA_ROPE), (MLA_NOPE, pad_head)))
    wkp = jnp.tile(place, (1, MLA_HEADS))
    wv = mla_w_ukv[:, :, MLA_NOPE:].reshape(MLA_KV_RANK, MLA_HEADS * MLA_V)
    gq = jnp.pad(mla_g_q, (0, pad_head)).reshape(1, LANES)
    gk = jnp.pad(mla_g_k, (0, pad_head)).reshape(1, LANES)
    sgq = jnp.tile(swa_g_q, 2).reshape(1, LANES)
    sgk = jnp.tile(swa_g_k, 2).reshape(1, LANES)
    n_a = MLA_HEADS * MLA_V
    w_out_b = ev_w_out[n_a:].reshape(SWA_HEADS, SWA_HEAD_DIM, D_MODEL)[order].reshape(nq, D_MODEL)
    wout = jnp.concatenate([ev_w_out[:n_a], w_out_b], axis=0)
    return dict(win=win.astype(BF16), wq=wq.astype(BF16), wkc=wkc.astype(BF16), wkp=wkp.astype(BF16),
                wv=wv.astype(BF16), gq=gq, gk=gk, sgq=sgq, sgk=sgk, wout=wout.astype(BF16))


def _peer_layer(x2d, xnt, w_query, sub_keys, u_emb, v_emb):
    wqt = w_query.reshape(D_MODEL, 2 * PEER_HEADS * PEER_HALF).T.astype(BF16)
    sk = sub_keys.reshape(2 * PEER_HEADS, PEER_N_KEYS, PEER_HALF).astype(BF16)
    return _peer_call(x2d, xnt, wqt, sk, u_emb.astype(BF16), v_emb.T.astype(BF16))


def kernel(x, t5_bias, norm_mix, norm_ffn, ev_w_in, ev_w_out, mla_g_cq, mla_w_uq, mla_g_ckv, mla_w_ukv,
           mla_g_q, mla_g_k, swa_g_q, swa_g_k, swa_sinks, od_w_in, od_w_out, ca_g_q, ca_g_k, ca_rel_bias,
           peer_w_query, peer_sub_keys, peer_u, peer_v):
    B, S, D = x.shape
    T = B * S
    x2d = x.reshape(T, D)
    row = lambda a: a.reshape(1, -1)

    w0 = _layer0_weights(ev_w_in[0], ev_w_out[0], mla_w_uq[0], mla_w_ukv[0], mla_g_q[0], mla_g_k[0],
                         swa_g_q[0], swa_g_k[0])
    rc, rs1, rs2 = _rope_tables(S)
    qm, km, vm, qs, ks, vs = _in0_call(
        x2d, S, row(norm_mix[0]), w0["win"], row(mla_g_cq[0]), w0["wq"], w0["gq"], row(mla_g_ckv[0]),
        w0["wkc"], w0["wkp"], w0["gk"], w0["wv"], w0["sgq"], w0["sgk"], rc, rs1, rs2)
    b3 = lambda a: a.reshape(B, S, a.shape[1])
    o_a = _mla_call(b3(qm), b3(km), b3(vm))
    o_b = _swa_call(swa_sinks[0], b3(qs), b3(ks), b3(vs), _swa_bias(t5_bias))
    o0 = jnp.concatenate([o_a, o_b], axis=-1).reshape(T, -1)
    x2d, xnt = _out_call(x2d, o0, w0["wout"], row(norm_ffn[0]))
    x2d = _peer_layer(x2d, xnt, peer_w_query[0], peer_sub_keys[0], peer_u[0], peer_v[0])

    gq2 = jnp.tile(ca_g_q[0], 2).reshape(1, LANES)
    gk2 = jnp.tile(ca_g_k[0], 2).reshape(1, LANES)
    qc, kc, vc = _in1_call(x2d, row(norm_mix[1]), od_w_in[0].astype(BF16), gq2, gk2)
    o_c = _ca_call(b3(qc), b3(kc), b3(vc), _ca_bias(ca_rel_bias[0]))
    x2d, xnt = _out_call(x2d, o_c.reshape(T, -1), od_w_out[0].astype(BF16), row(norm_ffn[1]))
    x2d = _peer_layer(x2d, xnt, peer_w_query[1], peer_sub_keys[1], peer_u[1], peer_v[1])
    return x2d.reshape(B, S, D)
```

```python
import functools
import math

import jax
import jax.numpy as jnp
import numpy as np
from jax import lax
from jax.experimental import pallas as pl
from jax.experimental.pallas import tpu as pltpu

F32 = jnp.float32
BF16 = jnp.bfloat16

D_MODEL = 1024
CHUNK = 64
EPS = 1e-6
NEG = -1e30
LANES = 128
HALF_LANES = LANES // 2

T5_BUCKETS = 32
T5_MAX_DIST = 128

MLA_HEADS = 8
MLA_NOPE = 64
MLA_ROPE = 32
MLA_V = 64
MLA_QK = MLA_NOPE + MLA_ROPE
MLA_Q_RANK = 384
MLA_KV_RANK = 256
ROPE_THETA = 10000.0

SWA_HEADS = 8
SWA_KV_HEADS = 2
SWA_GROUP = SWA_HEADS // SWA_KV_HEADS
SWA_HEAD_DIM = 64
SWA_BAND = 2

CA_HEADS = 16
CA_HEAD_DIM = 64
CA_LEFT_CHUNKS = 8
CA_REL_PAST = 256

PEER_HEADS = 8
PEER_N_KEYS = 128
PEER_HALF = 128
PEER_TOPK = 16

VMEM_LIMIT = 56 * 1024 * 1024

TM_PROJ = 256
TQ_MLA = 512
TQ_BAND = 128
TQ_CA = 256
TQ_PEER = 512
EB_PEER = 2048
TS_PEER = 256
QR_PEER = 16
KEY_GROUP = 8

GELU_C0 = math.sqrt(2.0 / math.pi)
GELU_C1 = GELU_C0 * 0.044715


def _cparams(sem):
    return pltpu.CompilerParams(dimension_semantics=sem, vmem_limit_bytes=VMEM_LIMIT)


def _rms(x, g):
    ms = jnp.mean(x * x, axis=-1, keepdims=True)
    return x * lax.rsqrt(ms + EPS) * g


def _rms_two_heads(x, g2, scale):
    lane = lax.broadcasted_iota(jnp.int32, x.shape, 1)
    lo_mask = lane < HALF_LANES
    s = x * x
    lo = jnp.sum(jnp.where(lo_mask, s, 0.0), axis=-1, keepdims=True)
    hi = jnp.sum(jnp.where(lo_mask, 0.0, s), axis=-1, keepdims=True)
    r = jnp.where(lo_mask, lax.rsqrt(lo * (1.0 / HALF_LANES) + EPS), lax.rsqrt(hi * (1.0 / HALF_LANES) + EPS))
    return x * r * (g2 * scale)


def _dot_nt(a, b):
    return lax.dot_general(a, b, (((1,), (1,)), ((), ())), preferred_element_type=F32)


def _in0_kernel(x_ref, gmix_ref, win_ref, gcq_ref, wq_ref, gq_ref, gckv_ref, wkc_ref, wkp_ref, gk_ref,
                wv_ref, sgq_ref, sgk_ref, rc_ref, rs1_ref, rs2_ref,
                qm_ref, km_ref, vm_ref, qs_ref, ks_ref, vs_ref):
    xn = _rms(x_ref[...], gmix_ref[...]).astype(BF16)
    p = jnp.dot(xn, win_ref[...], preferred_element_type=F32)
    cq = _rms(p[:, 0:MLA_Q_RANK], gcq_ref[...]).astype(BF16)
    ckv = _rms(p[:, 384:640], gckv_ref[...]).astype(BF16)
    kpe = p[:, 1408:1536].astype(BF16)
    q = jnp.dot(cq, wq_ref[...], preferred_element_type=F32)
    k = (jnp.dot(ckv, wkc_ref[...], preferred_element_type=F32)
         + jnp.dot(kpe, wkp_ref[...], preferred_element_type=F32))
    vm_ref[...] = jnp.dot(ckv, wv_ref[...], preferred_element_type=F32).astype(BF16)

    rc, rs1, rs2 = rc_ref[...], rs1_ref[...], rs2_ref[...]

    def head_norm_rope(t, g, scale):
        ms = jnp.sum(t * t, axis=-1, keepdims=True) * (1.0 / MLA_QK)
        tn = t * lax.rsqrt(ms + EPS) * g
        tr = tn * rc + pltpu.roll(tn, LANES - MLA_ROPE // 2, 1) * rs1 + pltpu.roll(tn, MLA_ROPE // 2, 1) * rs2
        return (tr * scale).astype(BF16)

    for h in range(MLA_HEADS):
        sl = slice(h * LANES, (h + 1) * LANES)
        qm_ref[:, sl] = head_norm_rope(q[:, sl], gq_ref[...], MLA_QK ** -0.5)
        km_ref[:, sl] = head_norm_rope(k[:, sl], gk_ref[...], 1.0)

    for m in range(SWA_HEADS // 2):
        sl = slice(640 + m * LANES, 640 + (m + 1) * LANES)
        qs_ref[:, m * LANES:(m + 1) * LANES] = _rms_two_heads(p[:, sl], sgq_ref[...], SWA_HEAD_DIM ** -0.5).astype(BF16)
    ks_ref[...] = _rms_two_heads(p[:, 1152:1280], sgk_ref[...], 1.0).astype(BF16)
    vs_ref[...] = p[:, 1280:1408].astype(BF16)


def _in0_call(x2d, seq, gmix, win, gcq, wq, gq, gckv, wkc, wkp, gk, wv, sgq, sgk, rc, rs1, rs2):
    T = x2d.shape[0]
    tm = TM_PROJ
    nblk_seq = seq // tm
    row = lambda i: (i, 0)
    const = lambda i: (0, 0)
    pos = lambda i: (i % nblk_seq, 0)
    full = lambda a: pl.BlockSpec(a.shape, const)
    out_shapes = (
        jax.ShapeDtypeStruct((T, MLA_HEADS * LANES), BF16),
        jax.ShapeDtypeStruct((T, MLA_HEADS * LANES), BF16),
        jax.ShapeDtypeStruct((T, MLA_HEADS * MLA_V), BF16),
        jax.ShapeDtypeStruct((T, SWA_HEADS * SWA_HEAD_DIM), BF16),
        jax.ShapeDtypeStruct((T, SWA_KV_HEADS * SWA_HEAD_DIM), BF16),
        jax.ShapeDtypeStruct((T, SWA_KV_HEADS * SWA_HEAD_DIM), BF16),
    )
    return pl.pallas_call(
        _in0_kernel,
        grid=(T // tm,),
        in_specs=[pl.BlockSpec((tm, D_MODEL), row), full(gmix), full(win), full(gcq), full(wq), full(gq),
                  full(gckv), full(wkc), full(wkp), full(gk), full(wv), full(sgq), full(sgk),
                  pl.BlockSpec((tm, LANES), pos), pl.BlockSpec((tm, LANES), pos), pl.BlockSpec((tm, LANES), pos)],
        out_specs=tuple(pl.BlockSpec((tm, s.shape[1]), row) for s in out_shapes),
        out_shape=out_shapes,
        compiler_params=_cparams(("parallel",)),
        name="layer0_in_proj",
    )(x2d, gmix, win, gcq, wq, gq, gckv, wkc, wkp, gk, wv, sgq, sgk, rc, rs1, rs2)


def _mla_kernel(q_ref, k_ref, v_ref, o_ref, m_scr, l_scr, acc_scr):
    i = pl.program_id(1)
    tq = TQ_MLA
    rows = lax.broadcasted_iota(jnp.int32, (tq, tq), 0)
    cols = lax.broadcasted_iota(jnp.int32, (tq, tq), 1)
    diag_ok = (cols // CHUNK) <= (rows // CHUNK)
    lane = lax.broadcasted_iota(jnp.int32, (tq, LANES), 1)
    m_scr[...] = jnp.full(m_scr.shape, NEG, F32)
    l_scr[...] = jnp.zeros_like(l_scr)
    acc_scr[...] = jnp.zeros_like(acc_scr)

    def body(j, _):
        start = pl.multiple_of(j * tq, tq)
        ok = jnp.logical_or(j < i, diag_ok)

        def scores(h):
            return _dot_nt(q_ref[0, :, h * LANES:(h + 1) * LANES],
                           k_ref[0, pl.ds(start, tq), h * LANES:(h + 1) * LANES])

        s_next = scores(0)
        for h in range(MLA_HEADS):
            s = jnp.where(ok, s_next, NEG)
            if h + 1 < MLA_HEADS:
                s_next = scores(h + 1)
            vb = v_ref[0, pl.ds(start, tq), (h // 2) * LANES:(h // 2 + 1) * LANES]
            m_old = m_scr[h]
            m_new = jnp.maximum(m_old, jnp.max(s, axis=-1, keepdims=True))
            alpha = jnp.exp(m_old - m_new)
            p = jnp.exp(s - m_new)
            l_scr[h] = alpha * l_scr[h] + jnp.sum(p, axis=-1, keepdims=True)
            acc_scr[h] = alpha * acc_scr[h] + jnp.dot(p.astype(BF16), vb, preferred_element_type=F32)
            m_scr[h] = m_new
        return 0

    lax.fori_loop(0, i + 1, body, 0)
    for hp in range(MLA_HEADS // 2):
        lo = acc_scr[2 * hp] / l_scr[2 * hp]
        hi = acc_scr[2 * hp + 1] / l_scr[2 * hp + 1]
        o_ref[0, :, hp * LANES:(hp + 1) * LANES] = jnp.where(lane < HALF_LANES, lo, hi).astype(BF16)


def _mla_call(q, k, v):
    B, S, _ = q.shape
    tq = TQ_MLA
    return pl.pallas_call(
        _mla_kernel,
        grid=(B, S // tq),
        in_specs=[pl.BlockSpec((1, tq, q.shape[2]), lambda b, i: (b, i, 0)),
                  pl.BlockSpec((1, S, k.shape[2]), lambda b, i: (b, 0, 0)),
                  pl.BlockSpec((1, S, v.shape[2]), lambda b, i: (b, 0, 0))],
        out_specs=pl.BlockSpec((1, tq, v.shape[2]), lambda b, i: (b, i, 0)),
        out_shape=jax.ShapeDtypeStruct((B, S, v.shape[2]), BF16),
        scratch_shapes=[pltpu.VMEM((MLA_HEADS, tq, 1), F32), pltpu.VMEM((MLA_HEADS, tq, 1), F32),
                        pltpu.VMEM((MLA_HEADS, tq, LANES), F32)],
        compiler_params=_cparams(("parallel", "arbitrary")),
        name="latent_attention",
    )(q, k, v)


def _swa_kernel(sink_ref, q_ref, kp_ref, kc_ref, vp_ref, vc_ref, bias_ref, o_ref):
    i = pl.program_id(1)
    tq = TQ_BAND
    kk = jnp.concatenate([kp_ref[0], kc_ref[0]], axis=0)
    vv = jnp.concatenate([vp_ref[0], vc_ref[0]], axis=0)
    lane = lax.broadcasted_iota(jnp.int32, (tq, LANES), 1)
    col = lax.broadcasted_iota(jnp.int32, (tq, 2 * tq), 1)
    key_ok = jnp.logical_or(col >= tq, i > 0)
    for m in range(SWA_HEADS // 2):
        qq = q_ref[0, :, m * LANES:(m + 1) * LANES]
        zero = jnp.zeros_like(qq)
        outs = []
        for hh in range(2):
            h = m + hh * SWA_GROUP
            qx = jnp.where(lane < HALF_LANES, qq, zero) if hh == 0 else jnp.where(lane < HALF_LANES, zero, qq)
            s = _dot_nt(qx, kk) + bias_ref[h]
            s = jnp.where(key_ok, s, NEG)
            sink = sink_ref[h]
            mx = jnp.maximum(jnp.max(s, axis=-1, keepdims=True), sink)
            p = jnp.exp(s - mx)
            den = jnp.sum(p, axis=-1, keepdims=True) + jnp.exp(sink - mx)
            o = jnp.dot(p.astype(BF16), vv, preferred_element_type=F32)
            outs.append(o / den)
        o_ref[0, :, m * LANES:(m + 1) * LANES] = jnp.where(lane < HALF_LANES, outs[0], outs[1]).astype(BF16)


def _swa_call(sinks, q, k, v, bias):
    B, S, _ = q.shape
    tq = TQ_BAND
    prev = lambda b, i: (b, jnp.maximum(i - 1, 0), 0)
    cur = lambda b, i: (b, i, 0)
    return pl.pallas_call(
        _swa_kernel,
        grid=(B, S // tq),
        in_specs=[pl.BlockSpec(memory_space=pltpu.SMEM),
                  pl.BlockSpec((1, tq, q.shape[2]), cur),
                  pl.BlockSpec((1, tq, LANES), prev), pl.BlockSpec((1, tq, LANES), cur),
                  pl.BlockSpec((1, tq, LANES), prev), pl.BlockSpec((1, tq, LANES), cur),
                  pl.BlockSpec(bias.shape, lambda b, i: (0, 0, 0))],
        out_specs=pl.BlockSpec((1, tq, q.shape[2]), cur),
        out_shape=jax.ShapeDtypeStruct(q.shape, BF16),
        compiler_params=_cparams(("parallel", "arbitrary")),
        name="sliding_window_attention",
    )(sinks, q, k, k, v, v, bias)


CA_KBLOCKS = CA_LEFT_CHUNKS * CHUNK // TQ_CA + 1


def _ca_kernel(q_ref, k_ref, v_ref, bias_ref, o_ref):
    i = pl.program_id(1)
    tq = TQ_CA
    nk = CA_KBLOCKS * tq
    lane = lax.broadcasted_iota(jnp.int32, (tq, LANES), 1)
    col = lax.broadcasted_iota(jnp.int32, (tq, nk), 1)
    key_ok = (col // tq) >= (CA_KBLOCKS - 1 - i)
    starts = [pl.multiple_of(jnp.maximum(i - (CA_KBLOCKS - 1) + j, 0) * tq, tq) for j in range(CA_KBLOCKS)]

    def scores(h):
        sl = slice((h // 2) * LANES, (h // 2 + 1) * LANES)
        kk = jnp.concatenate([k_ref[0, pl.ds(st, tq), sl] for st in starts], axis=0)
        qq = q_ref[0, :, sl]
        zero = jnp.zeros_like(qq)
        qx = jnp.where(lane < HALF_LANES, qq, zero) if h % 2 == 0 else jnp.where(lane < HALF_LANES, zero, qq)
        return _dot_nt(qx, kk)

    s_next = scores(0)
    outs = []
    for h in range(CA_HEADS):
        sl = slice((h // 2) * LANES, (h // 2 + 1) * LANES)
        s = jnp.where(key_ok, s_next + bias_ref[h], NEG)
        if h + 1 < CA_HEADS:
            s_next = scores(h + 1)
        vv = jnp.concatenate([v_ref[0, pl.ds(st, tq), sl] for st in starts], axis=0)
        mx = jnp.max(s, axis=-1, keepdims=True)
        p = jnp.exp(s - mx)
        den = jnp.sum(p, axis=-1, keepdims=True)
        o = jnp.dot(p.astype(BF16), vv, preferred_element_type=F32)
        outs.append(o / den)
        if h % 2 == 1:
            o_ref[0, :, sl] = jnp.where(lane < HALF_LANES, outs[0], outs[1]).astype(BF16)
            outs = []


def _ca_call(q, k, v, bias):
    B, S, W = q.shape
    tq = TQ_CA
    return pl.pallas_call(
        _ca_kernel,
        grid=(B, S // tq),
        in_specs=[pl.BlockSpec((1, tq, W), lambda b, i: (b, i, 0)),
                  pl.BlockSpec((1, S, W), lambda b, i: (b, 0, 0)),
                  pl.BlockSpec((1, S, W), lambda b, i: (b, 0, 0)),
                  pl.BlockSpec(bias.shape, lambda b, i: (0, 0, 0))],
        out_specs=pl.BlockSpec((1, tq, W), lambda b, i: (b, i, 0)),
        out_shape=jax.ShapeDtypeStruct(q.shape, BF16),
        compiler_params=_cparams(("parallel", "arbitrary")),
        name="chunk_attention",
    )(q, k, v, bias)


def _in1_kernel(x_ref, gmix_ref, win_ref, gq_ref, gk_ref, q_ref, k_ref, v_ref):
    xn = _rms(x_ref[...], gmix_ref[...]).astype(BF16)
    p = jnp.dot(xn, win_ref[...], preferred_element_type=F32)
    w = CA_HEADS * CA_HEAD_DIM
    for m in range(CA_HEADS // 2):
        sl = slice(m * LANES, (m + 1) * LANES)
        q_ref[:, sl] = _rms_two_heads(p[:, m * LANES:(m + 1) * LANES], gq_ref[...], CA_HEAD_DIM ** -0.5).astype(BF16)
        k_ref[:, sl] = _rms_two_heads(p[:, w + m * LANES:w + (m + 1) * LANES], gk_ref[...], 1.0).astype(BF16)
    v_ref[...] = p[:, 2 * w:3 * w].astype(BF16)


def _in1_call(x2d, gmix, win, gq, gk):
    T = x2d.shape[0]
    tm = TM_PROJ
    w = CA_HEADS * CA_HEAD_DIM
    row = lambda i: (i, 0)
    const = lambda i: (0, 0)
    full = lambda a: pl.BlockSpec(a.shape, const)
    out_shapes = tuple(jax.ShapeDtypeStruct((T, w), BF16) for _ in range(3))
    return pl.pallas_call(
        _in1_kernel,
        grid=(T // tm,),
        in_specs=[pl.BlockSpec((tm, D_MODEL), row), full(gmix), full(win), full(gq), full(gk)],
        out_specs=tuple(pl.BlockSpec((tm, w), row) for _ in range(3)),
        out_shape=out_shapes,
        compiler_params=_cparams(("parallel",)),
        name="layer1_in_proj",
    )(x2d, gmix, win, gq, gk)


def _out_kernel(x_ref, o_ref, w_ref, g_ref, xo_ref, xnt_ref):
    y = x_ref[...] + jnp.dot(o_ref[...], w_ref[...], preferred_element_type=F32)
    xo_ref[...] = y
    xnt_ref[...] = _rms(y, g_ref[...]).T.astype(BF16)


def _out_call(x2d, o2d, w, g):
    T = x2d.shape[0]
    tm = TM_PROJ
    row = lambda i: (i, 0)
    const = lambda i: (0, 0)
    return pl.pallas_call(
        _out_kernel,
        grid=(T // tm,),
        in_specs=[pl.BlockSpec((tm, D_MODEL), row), pl.BlockSpec((tm, o2d.shape[1]), row),
                  pl.BlockSpec(w.shape, const), pl.BlockSpec(g.shape, const)],
        out_specs=(pl.BlockSpec((tm, D_MODEL), row), pl.BlockSpec((D_MODEL, tm), lambda i: (0, i))),
        out_shape=(jax.ShapeDtypeStruct((T, D_MODEL), F32), jax.ShapeDtypeStruct((D_MODEL, T), BF16)),
        compiler_params=_cparams(("parallel",)),
        name="out_proj_residual",
    )(x2d, o2d, w, g)


NTB_PEER = TQ_PEER // LANES
I1_PER_STEP = EB_PEER // PEER_N_KEYS


SUBLANES = 8
BIG = 3.0e38

SORT16 = (
    (0, 13), (1, 12), (2, 15), (3, 14), (4, 8), (5, 6), (7, 11), (9, 10),
    (0, 5), (1, 7), (2, 9), (3, 4), (6, 13), (8, 14), (10, 15), (11, 12),
    (0, 1), (2, 3), (4, 5), (6, 8), (7, 9), (10, 11), (12, 13), (14, 15),
    (0, 2), (1, 3), (4, 10), (5, 11), (6, 7), (8, 9), (12, 14), (13, 15),
    (1, 2), (3, 12), (4, 6), (5, 7), (8, 10), (9, 11), (13, 14),
    (1, 4), (2, 6), (5, 8), (7, 10), (9, 13), (11, 14),
    (2, 4), (3, 6), (9, 12), (11, 13),
    (3, 5), (6, 8), (7, 9), (10, 12),
    (3, 4), (5, 6), (7, 8), (9, 10), (11, 12),
    (6, 7), (8, 9),
)
BITONIC16 = tuple((r, r + st) for st in (8, 4, 2, 1) for r in range(16) if (r // st) % 2 == 0)


def _compare_exchange(v, i, j):
    hi, lo = jnp.maximum(v[i], v[j]), jnp.minimum(v[i], v[j])
    v[i], v[j] = hi, lo


def _merge_across_sublanes(v):
    for shift in (4, 2, 1):
        y = [pltpu.roll(t, shift, 0) for t in v]
        v = [jnp.maximum(v[r], y[PEER_TOPK - 1 - r]) for r in range(PEER_TOPK)]
        for i, j in BITONIC16:
            _compare_exchange(v, i, j)
    return v


def _top16(x):
    v = [x[SUBLANES * i:SUBLANES * (i + 1), :] for i in range(PEER_N_KEYS // SUBLANES)]
    for i, j in SORT16:
        _compare_exchange(v, i, j)
    return _merge_across_sublanes(v)


def _route_tile(s1, s2):
    ta = _top16(s1)
    tb = _top16(s2)
    sub = lax.broadcasted_iota(jnp.int32, (SUBLANES, LANES), 0)
    a8 = ta[SUBLANES - 1]
    for j in range(SUBLANES - 2, -1, -1):
        a8 = jnp.where(sub == j, ta[j], a8)
    cl = []
    for k in range(PEER_TOPK):
        nj = PEER_TOPK // (k + 1)
        l = a8 + tb[k]
        cl.append(l if nj >= SUBLANES else jnp.where(sub < nj, l, NEG))
    m = _merge_across_sublanes(list(cl))
    ex = [ta[SUBLANES + r] + tb[0] for r in range(SUBLANES)]
    top = m[:SUBLANES] + [jnp.maximum(m[r], ex[PEER_TOPK - 1 - r]) for r in range(SUBLANES, PEER_TOPK)]
    for i, j in BITONIC16:
        _compare_exchange(top, i, j)
    tau = top[PEER_TOPK - 1]
    z = jnp.exp(top[0] - top[0])
    for r in range(1, PEER_TOPK):
        z = z + jnp.exp(top[r] - top[0])
    rz = 0.5 / z

    e2k = [jnp.exp(tb[k] - tb[0]) for k in range(PEER_TOPK)]
    e1top = jnp.exp(a8 - ta[0]) * rz
    thr = jnp.full((SUBLANES, LANES), BIG, F32)
    for k in range(PEER_TOPK):
        thr = jnp.minimum(thr, jnp.where(cl[k] >= tau, e1top * e2k[k], BIG))
    for r in range(SUBLANES):
        e1x = jnp.exp(ta[SUBLANES + r] - ta[0]) * rz
        thr = jnp.minimum(thr, jnp.where(ex[r] >= tau, e1x * e2k[0], BIG))
    for shift in (4, 2, 1):
        thr = jnp.minimum(thr, pltpu.roll(thr, shift, 0))
    n = PEER_N_KEYS // SUBLANES
    e1 = jnp.concatenate([jnp.exp(s1[SUBLANES * i:SUBLANES * (i + 1), :] - ta[0]) * rz for i in range(n)], axis=0)
    e2 = jnp.concatenate([jnp.exp(s2[SUBLANES * i:SUBLANES * (i + 1), :] - tb[0]) for i in range(n)], axis=0)
    return e1, e2, thr


def _peer_kernel(x_ref, xnt_ref, wqt_ref, sk_ref, u_ref, vt_ref, o_ref,
                 q_scr, sc_scr, e1_scr, e2_scr, thr_scr, a_scr, g_scr, acc_scr):
    e = pl.program_id(1)

    @pl.when(e == 0)
    def _route():
        q_scr[...] = jnp.dot(wqt_ref[...], xnt_ref[...], preferred_element_type=F32).astype(BF16)
        acc_scr[...] = jnp.zeros_like(acc_scr)

        def per_head(h, _):
            for p in range(2):
                hp = h * 2 + p
                qs = q_scr[pl.ds(pl.multiple_of(hp * PEER_HALF, PEER_HALF), PEER_HALF), :]
                sc = jnp.dot(sk_ref[hp], qs, preferred_element_type=F32)
                for tb in range(NTB_PEER):
                    sc_scr[p, tb] = sc[:, tb * LANES:(tb + 1) * LANES]

            def per_tile(tb, _):
                e1, e2, thr = _route_tile(sc_scr[0, tb], sc_scr[1, tb])
                e1_scr[h, tb] = e1
                e2_scr[h, tb] = e2
                thr_scr[h, tb] = thr
                return 0

            lax.fori_loop(0, NTB_PEER, per_tile, 0)
            return 0

        lax.fori_loop(0, PEER_HEADS, per_head, 0)

    n_sub = TQ_PEER // TS_PEER

    def route_weights(s, key0):
        for tb2 in range(TS_PEER // LANES):
            tb = s * (TS_PEER // LANES) + tb2
            ts = slice(tb2 * LANES, (tb2 + 1) * LANES)
            for q in range(PEER_N_KEYS // QR_PEER):
                wsum = [jnp.zeros((QR_PEER, LANES), F32) for _ in range(KEY_GROUP)]
                for h in range(PEER_HEADS):
                    e2q = jnp.maximum(e2_scr[h, tb, q * QR_PEER:(q + 1) * QR_PEER, :], 0.0)
                    thr = thr_scr[h, tb, 0:1, :]
                    for r in range(KEY_GROUP):
                        w = e2q * e1_scr[h, tb, pl.ds(e * I1_PER_STEP + key0 + r, 1), :]
                        wsum[r] = wsum[r] + jnp.where(w >= thr, w, 0.0)
                for r in range(KEY_GROUP):
                    row0 = (key0 + r) * PEER_N_KEYS + q * QR_PEER
                    rows = slice(row0, row0 + QR_PEER)
                    at = a_scr[s, rows, ts]
                    g = wsum[r] * at * (1.0 + jnp.tanh(at * (GELU_C0 + GELU_C1 * (at * at))))
                    g_scr[s, rows, ts] = g.astype(BF16)

    for s in range(n_sub):
        a_scr[s] = jnp.dot(u_ref[...], xnt_ref[:, s * TS_PEER:(s + 1) * TS_PEER],
                           preferred_element_type=F32)
    for s in range(n_sub):
        for key0 in range(0, I1_PER_STEP, KEY_GROUP):
            route_weights(s, key0)
        acc_scr[:, s * TS_PEER:(s + 1) * TS_PEER] += jnp.dot(vt_ref[...], g_scr[s], preferred_element_type=F32)

    @pl.when(e == pl.num_programs(1) - 1)
    def _finish():
        o_ref[...] = x_ref[...] + acc_scr[...].T


def _peer_call(x2d, xnt, wqt, sk, u, vt):
    T = x2d.shape[0]
    tq, eb = TQ_PEER, EB_PEER
    n_exp = u.shape[0]
    return pl.pallas_call(
        _peer_kernel,
        grid=(T // tq, n_exp // eb),
        in_specs=[pl.BlockSpec((tq, D_MODEL), lambda t, e: (t, 0)),
                  pl.BlockSpec((D_MODEL, tq), lambda t, e: (0, t)),
                  pl.BlockSpec(wqt.shape, lambda t, e: (0, 0)),
                  pl.BlockSpec(sk.shape, lambda t, e: (0, 0, 0)),
                  pl.BlockSpec((eb, D_MODEL), lambda t, e: (e, 0)),
                  pl.BlockSpec((D_MODEL, eb), lambda t, e: (0, e))],
        out_specs=pl.BlockSpec((tq, D_MODEL), lambda t, e: (t, 0)),
        out_shape=jax.ShapeDtypeStruct((T, D_MODEL), F32),
        scratch_shapes=[
            pltpu.VMEM((2 * PEER_HEADS * PEER_HALF, tq), BF16),
            pltpu.VMEM((2, NTB_PEER, PEER_N_KEYS, LANES), F32),
            pltpu.VMEM((PEER_HEADS, NTB_PEER, PEER_N_KEYS, LANES), F32),
            pltpu.VMEM((PEER_HEADS, NTB_PEER, PEER_N_KEYS, LANES), F32),
            pltpu.VMEM((PEER_HEADS, NTB_PEER, SUBLANES, LANES), F32),
            pltpu.VMEM((tq // TS_PEER, eb, TS_PEER), F32),
            pltpu.VMEM((tq // TS_PEER, eb, TS_PEER), BF16),
            pltpu.VMEM((D_MODEL, tq), F32),
        ],
        compiler_params=_cparams(("parallel", "arbitrary")),
        name="peer_dense",
    )(x2d, xnt, wqt, sk, u, vt)


def _t5_bucket(rel):
    nb = T5_BUCKETS // 2
    max_exact = nb // 2
    ret = jnp.where(rel > 0, nb, 0)
    n = jnp.abs(rel)
    large = max_exact + (jnp.log(jnp.maximum(n, 1).astype(F32) / max_exact)
                         / math.log(T5_MAX_DIST / max_exact) * (nb - max_exact)).astype(jnp.int32)
    large = jnp.minimum(large, nb - 1)
    return ret + jnp.where(n < max_exact, n, large)


def _toeplitz(f, n_rows, n_cols):
    return jnp.stack([f[:, n_rows - 1 - r:n_rows - 1 - r + n_cols] for r in range(n_rows)], axis=1)


def _swa_bias(t5_table):
    tq = TQ_BAND
    r = jnp.arange(tq)[:, None]
    c = jnp.arange(2 * tq)[None, :]
    rel_line = jnp.arange(3 * tq - 1) - (2 * tq - 1)
    bias = _toeplitz(t5_table[_t5_bucket(rel_line)].astype(F32).T, tq, 2 * tq)
    qc = (tq + r) // CHUNK
    kc = c // CHUNK
    vis = jnp.logical_and(kc >= qc - SWA_BAND, kc <= qc)
    return jnp.where(vis[None], bias, NEG)


def _ca_bias(rel_table):
    tq = TQ_CA
    left = CA_LEFT_CHUNKS * CHUNK
    r = jnp.arange(tq)[:, None]
    c = jnp.arange(left + tq)[None, :]
    rel_line = jnp.arange(left + 2 * tq - 1) - (left + tq - 1)
    idx = jnp.clip(rel_line, -CA_REL_PAST, CHUNK - 1) + CA_REL_PAST
    bias = _toeplitz(rel_table[:, idx].astype(F32), tq, left + tq)
    qc = (left + r) // CHUNK
    kc = c // CHUNK
    vis = jnp.logical_and(kc >= qc - CA_LEFT_CHUNKS, kc <= qc)
    return jnp.where(vis[None], bias, NEG)


def _rope_tables(seq):
    half = MLA_ROPE // 2
    inv = ROPE_THETA ** (-jnp.arange(0, MLA_ROPE, 2, dtype=F32) / MLA_ROPE)
    ang = jnp.arange(seq, dtype=F32)[:, None] * inv[None, :]
    cos, sin = jnp.cos(ang), jnp.sin(ang)
    z = lambda n: jnp.zeros((seq, n), F32)
    tail = LANES - MLA_QK
    rc = jnp.concatenate([jnp.ones((seq, MLA_NOPE), F32), cos, cos, z(tail)], axis=1)
    rs1 = jnp.concatenate([z(MLA_NOPE), -sin, z(half), z(tail)], axis=1)
    rs2 = jnp.concatenate([z(MLA_NOPE), z(half), sin, z(tail)], axis=1)
    return rc, rs1, rs2


SWA_HEAD_ORDER = tuple(h for m in range(SWA_GROUP) for h in (m, m + SWA_GROUP))


def _layer0_weights(ev_w_in, ev_w_out, mla_w_uq, mla_w_ukv, mla_g_q, mla_g_k, swa_g_q, swa_g_k):
    pad_head = LANES - MLA_QK
    mla_in = MLA_Q_RANK + MLA_KV_RANK + MLA_ROPE
    nq = SWA_HEADS * SWA_HEAD_DIM
    nkv = SWA_KV_HEADS * SWA_HEAD_DIM
    order = jnp.asarray(SWA_HEAD_ORDER)
    w_cq_ckv = ev_w_in[:, :MLA_Q_RANK + MLA_KV_RANK]
    w_kpe = ev_w_in[:, MLA_Q_RANK + MLA_KV_RANK:mla_in]
    w_sq = ev_w_in[:, mla_in:mla_in + nq].reshape(D_MODEL, SWA_HEADS, SWA_HEAD_DIM)[:, order].reshape(D_MODEL, nq)
    w_skv = ev_w_in[:, mla_in + nq:mla_in + nq + 2 * nkv]
    win = jnp.concatenate([w_cq_ckv, w_sq, w_skv, w_kpe, jnp.zeros((D_MODEL, LANES - MLA_ROPE), F32)], axis=1)
    wq = jnp.pad(mla_w_uq, ((0, 0), (0, 0), (0, pad_head))).reshape(MLA_Q_RANK, MLA_HEADS * LANES)
    wkc = jnp.pad(mla_w_ukv[:, :, :MLA_NOPE], ((0, 0), (0, 0), (0, LANES - MLA_NOPE))).reshape(MLA_KV_RANK, MLA_HEADS * LANES)
    place = jnp.pad(jnp.eye(MLA_ROPE, dtype=F32), ((0, LANES - MLA_ROPE), (MLA_NOPE, pad_head)))
    wkp = jnp.tile(place, (1, MLA_HEADS))
    wv = mla_w_ukv[:, :, MLA_NOPE:].reshape(MLA_KV_RANK, MLA_HEADS * MLA_V)
    gq = jnp.pad(mla_g_q, (0, pad_head)).reshape(1, LANES)
    gk = jnp.pad(mla_g_k, (0, pad_head)).reshape(1, LANES)
    sgq = jnp.tile(swa_g_q, 2).reshape(1, LANES)
    sgk = jnp.tile(swa_g_k, 2).reshape(1, LANES)
    n_a = MLA_HEADS * MLA_V
    w_out_b = ev_w_out[n_a:].reshape(SWA_HEADS, SWA_HEAD_DIM, D_MODEL)[order].reshape(nq, D_MODEL)
    wout = jnp.concatenate([ev_w_out[:n_a], w_out_b], axis=0)
    return dict(win=win.astype(BF16), wq=wq.astype(BF16), wkc=wkc.astype(BF16), wkp=wkp.astype(BF16),
                wv=wv.astype(BF16), gq=gq, gk=gk, sgq=sgq, sgk=sgk, wout=wout.astype(BF16))


def _peer_layer(x2d, xnt, w_query, sub_keys, u_emb, v_emb):
    wqt = w_query.reshape(D_MODEL, 2 * PEER_HEADS * PEER_HALF).T.astype(BF16)
    sk = sub_keys.reshape(2 * PEER_HEADS, PEER_N_KEYS, PEER_HALF).astype(BF16)
    return _peer_call(x2d, xnt, wqt, sk, u_emb.astype(BF16), v_emb.T.astype(BF16))


def kernel(x, t5_bias, norm_mix, norm_ffn, ev_w_in, ev_w_out, mla_g_cq, mla_w_uq, mla_g_ckv, mla_w_ukv,
           mla_g_q, mla_g_k, swa_g_q, swa_g_k, swa_sinks, od_w_in, od_w_out, ca_g_q, ca_g_k, ca_rel_bias,
           peer_w_query, peer_sub_keys, peer_u, peer_v):
    B, S, D = x.shape
    T = B * S
    x2d = x.reshape(T, D)
    row = lambda a: a.reshape(1, -1)

    w0 = _layer0_weights(ev_w_in[0], ev_w_out[0], mla_w_uq[0], mla_w_ukv[0], mla_g_q[0], mla_g_k[0],
                         swa_g_q[0], swa_g_k[0])
    rc, rs1, rs2 = _rope_tables(S)
    qm, km, vm, qs, ks, vs = _in0_call(
        x2d, S, row(norm_mix[0]), w0["win"], row(mla_g_cq[0]), w0["wq"], w0["gq"], row(mla_g_ckv[0]),
        w0["wkc"], w0["wkp"], w0["gk"], w0["wv"], w0["sgq"], w0["sgk"], rc, rs1, rs2)
    b3 = lambda a: a.reshape(B, S, a.shape[1])
    o_a = _mla_call(b3(qm), b3(km), b3(vm))
    o_b = _swa_call(swa_sinks[0], b3(qs), b3(ks), b3(vs), _swa_bias(t5_bias))
    o0 = jnp.concatenate([o_a, o_b], axis=-1).reshape(T, -1)
    x2d, xnt = _out_call(x2d, o0, w0["wout"], row(norm_ffn[0]))
    x2d = _peer_layer(x2d, xnt, peer_w_query[0], peer_sub_keys[0], peer_u[0], peer_v[0])

    gq2 = jnp.tile(ca_g_q[0], 2).reshape(1, LANES)
    gk2 = jnp.tile(ca_g_k[0], 2).reshape(1, LANES)
    qc, kc, vc = _in1_call(x2d, row(norm_mix[1]), od_w_in[0].astype(BF16), gq2, gk2)
    o_c = _ca_call(b3(qc), b3(kc), b3(vc), _ca_bias(ca_rel_bias[0]))
    x2d, xnt = _out_call(x2d, o_c.reshape(T, -1), od_w_out[0].astype(BF16), row(norm_ffn[1]))
    x2d = _peer_layer(x2d, xnt, peer_w_query[1], peer_sub_keys[1], peer_u[1], peer_v[1])
    return x2d.reshape(B, S, D)
```

```python
import functools
import math

import jax
import jax.numpy as jnp
import numpy as np
from jax import lax
from jax.experimental import pallas as pl
from jax.experimental.pallas import tpu as pltpu

F32 = jnp.float32
BF16 = jnp.bfloat16

D_MODEL = 1024
CHUNK = 64
EPS = 1e-6
NEG = -1e30
LANES = 128
HALF_LANES = LANES // 2

T5_BUCKETS = 32
T5_MAX_DIST = 128

MLA_HEADS = 8
MLA_NOPE = 64
MLA_ROPE = 32
MLA_V = 64
MLA_QK = MLA_NOPE + MLA_ROPE
MLA_Q_RANK = 384
MLA_KV_RANK = 256
ROPE_THETA = 10000.0

SWA_HEADS = 8
SWA_KV_HEADS = 2
SWA_GROUP = SWA_HEADS // SWA_KV_HEADS
SWA_HEAD_DIM = 64
SWA_BAND = 2

CA_HEADS = 16
CA_HEAD_DIM = 64
CA_LEFT_CHUNKS = 8
CA_REL_PAST = 256

PEER_HEADS = 8
PEER_N_KEYS = 128
PEER_HALF = 128
PEER_TOPK = 16

VMEM_LIMIT = 56 * 1024 * 1024

TM_PROJ = 256
TQ_MLA = 512
TQ_BAND = 128
TQ_CA = 256
TQ_PEER = 512
EB_PEER = 2048
TS_PEER = 256
QR_PEER = 16
KEY_GROUP = 8

GELU_C0 = math.sqrt(2.0 / math.pi)
GELU_C1 = GELU_C0 * 0.044715


def _cparams(sem):
    return pltpu.CompilerParams(dimension_semantics=sem, vmem_limit_bytes=VMEM_LIMIT)


def _rms(x, g):
    ms = jnp.mean(x * x, axis=-1, keepdims=True)
    return x * lax.rsqrt(ms + EPS) * g


def _rms_two_heads(x, g2, scale):
    lane = lax.broadcasted_iota(jnp.int32, x.shape, 1)
    lo_mask = lane < HALF_LANES
    s = x * x
    lo = jnp.sum(jnp.where(lo_mask, s, 0.0), axis=-1, keepdims=True)
    hi = jnp.sum(jnp.where(lo_mask, 0.0, s), axis=-1, keepdims=True)
    r = jnp.where(lo_mask, lax.rsqrt(lo * (1.0 / HALF_LANES) + EPS), lax.rsqrt(hi * (1.0 / HALF_LANES) + EPS))
    return x * r * (g2 * scale)


def _dot_nt(a, b):
    return lax.dot_general(a, b, (((1,), (1,)), ((), ())), preferred_element_type=F32)


def _in0_kernel(x_ref, gmix_ref, win_ref, gcq_ref, wq_ref, gq_ref, gckv_ref, wkc_ref, wkp_ref, gk_ref,
                wv_ref, sgq_ref, sgk_ref, rc_ref, rs1_ref, rs2_ref,
                qm_ref, km_ref, vm_ref, qs_ref, ks_ref, vs_ref):
    xn = _rms(x_ref[...], gmix_ref[...]).astype(BF16)
    p = jnp.dot(xn, win_ref[...], preferred_element_type=F32)
    cq = _rms(p[:, 0:MLA_Q_RANK], gcq_ref[...]).astype(BF16)
    ckv = _rms(p[:, 384:640], gckv_ref[...]).astype(BF16)
    kpe = p[:, 1408:1536].astype(BF16)
    q = jnp.dot(cq, wq_ref[...], preferred_element_type=F32)
    k = (jnp.dot(ckv, wkc_ref[...], preferred_element_type=F32)
         + jnp.dot(kpe, wkp_ref[...], preferred_element_type=F32))
    vm_ref[...] = jnp.dot(ckv, wv_ref[...], preferred_element_type=F32).astype(BF16)

    rc, rs1, rs2 = rc_ref[...], rs1_ref[...], rs2_ref[...]

    def head_norm_rope(t, g, scale):
        ms = jnp.sum(t * t, axis=-1, keepdims=True) * (1.0 / MLA_QK)
        tn = t * lax.rsqrt(ms + EPS) * g
        tr = tn * rc + pltpu.roll(tn, LANES - MLA_ROPE // 2, 1) * rs1 + pltpu.roll(tn, MLA_ROPE // 2, 1) * rs2
        return (tr * scale).astype(BF16)

    for h in range(MLA_HEADS):
        sl = slice(h * LANES, (h + 1) * LANES)
        qm_ref[:, sl] = head_norm_rope(q[:, sl], gq_ref[...], MLA_QK ** -0.5)
        km_ref[:, sl] = head_norm_rope(k[:, sl], gk_ref[...], 1.0)

    for m in range(SWA_HEADS // 2):
        sl = slice(640 + m * LANES, 640 + (m + 1) * LANES)
        qs_ref[:, m * LANES:(m + 1) * LANES] = _rms_two_heads(p[:, sl], sgq_ref[...], SWA_HEAD_DIM ** -0.5).astype(BF16)
    ks_ref[...] = _rms_two_heads(p[:, 1152:1280], sgk_ref[...], 1.0).astype(BF16)
    vs_ref[...] = p[:, 1280:1408].astype(BF16)


def _in0_call(x2d, seq, gmix, win, gcq, wq, gq, gckv, wkc, wkp, gk, wv, sgq, sgk, rc, rs1, rs2):
    T = x2d.shape[0]
    tm = TM_PROJ
    nblk_seq = seq // tm
    row = lambda i: (i, 0)
    const = lambda i: (0, 0)
    pos = lambda i: (i % nblk_seq, 0)
    full = lambda a: pl.BlockSpec(a.shape, const)
    out_shapes = (
        jax.ShapeDtypeStruct((T, MLA_HEADS * LANES), BF16),
        jax.ShapeDtypeStruct((T, MLA_HEADS * LANES), BF16),
        jax.ShapeDtypeStruct((T, MLA_HEADS * MLA_V), BF16),
        jax.ShapeDtypeStruct((T, SWA_HEADS * SWA_HEAD_DIM), BF16),
        jax.ShapeDtypeStruct((T, SWA_KV_HEADS * SWA_HEAD_DIM), BF16),
        jax.ShapeDtypeStruct((T, SWA_KV_HEADS * SWA_HEAD_DIM), BF16),
    )
    return pl.pallas_call(
        _in0_kernel,
        grid=(T // tm,),
        in_specs=[pl.BlockSpec((tm, D_MODEL), row), full(gmix), full(win), full(gcq), full(wq), full(gq),
                  full(gckv), full(wkc), full(wkp), full(gk), full(wv), full(sgq), full(sgk),
                  pl.BlockSpec((tm, LANES), pos), pl.BlockSpec((tm, LANES), pos), pl.BlockSpec((tm, LANES), pos)],
        out_specs=tuple(pl.BlockSpec((tm, s.shape[1]), row) for s in out_shapes),
        out_shape=out_shapes,
        compiler_params=_cparams(("parallel",)),
        name="layer0_in_proj",
    )(x2d, gmix, win, gcq, wq, gq, gckv, wkc, wkp, gk, wv, sgq, sgk, rc, rs1, rs2)


def _mla_kernel(q_ref, k_ref, v_ref, o_ref, m_scr, l_scr, acc_scr):
    i = pl.program_id(1)
    tq = TQ_MLA
    rows = lax.broadcasted_iota(jnp.int32, (tq, tq), 0)
    cols = lax.broadcasted_iota(jnp.int32, (tq, tq), 1)
    diag_ok = (cols // CHUNK) <= (rows // CHUNK)
    lane = lax.broadcasted_iota(jnp.int32, (tq, LANES), 1)
    m_scr[...] = jnp.full(m_scr.shape, NEG, F32)
    l_scr[...] = jnp.zeros_like(l_scr)
    acc_scr[...] = jnp.zeros_like(acc_scr)

    def body(j, _):
        start = pl.multiple_of(j * tq, tq)
        ok = jnp.logical_or(j < i, diag_ok)

        def scores(h):
            return _dot_nt(q_ref[0, :, h * LANES:(h + 1) * LANES],
                           k_ref[0, pl.ds(start, tq), h * LANES:(h + 1) * LANES])

        s_next = scores(0)
        for h in range(MLA_HEADS):
            s = jnp.where(ok, s_next, NEG)
            if h + 1 < MLA_HEADS:
                s_next = scores(h + 1)
            vb = v_ref[0, pl.ds(start, tq), (h // 2) * LANES:(h // 2 + 1) * LANES]
            m_old = m_scr[h]
            m_new = jnp.maximum(m_old, jnp.max(s, axis=-1, keepdims=True))
            alpha = jnp.exp(m_old - m_new)
            p = jnp.exp(s - m_new)
            l_scr[h] = alpha * l_scr[h] + jnp.sum(p, axis=-1, keepdims=True)
            acc_scr[h] = alpha * acc_scr[h] + jnp.dot(p.astype(BF16), vb, preferred_element_type=F32)
            m_scr[h] = m_new
        return 0

    lax.fori_loop(0, i + 1, body, 0)
    for hp in range(MLA_HEADS // 2):
        lo = acc_scr[2 * hp] / l_scr[2 * hp]
        hi = acc_scr[2 * hp + 1] / l_scr[2 * hp + 1]
        o_ref[0, :, hp * LANES:(hp + 1) * LANES] = jnp.where(lane < HALF_LANES, lo, hi).astype(BF16)


def _mla_call(q, k, v):
    B, S, _ = q.shape
    tq = TQ_MLA
    return pl.pallas_call(
        _mla_kernel,
        grid=(B, S // tq),
        in_specs=[pl.BlockSpec((1, tq, q.shape[2]), lambda b, i: (b, i, 0)),
                  pl.BlockSpec((1, S, k.shape[2]), lambda b, i: (b, 0, 0)),
                  pl.BlockSpec((1, S, v.shape[2]), lambda b, i: (b, 0, 0))],
        out_specs=pl.BlockSpec((1, tq, v.shape[2]), lambda b, i: (b, i, 0)),
        out_shape=jax.ShapeDtypeStruct((B, S, v.shape[2]), BF16),
        scratch_shapes=[pltpu.VMEM((MLA_HEADS, tq, 1), F32), pltpu.VMEM((MLA_HEADS, tq, 1), F32),
                        pltpu.VMEM((MLA_HEADS, tq, LANES), F32)],
        compiler_params=_cparams(("parallel", "arbitrary")),
        name="latent_attention",
    )(q, k, v)


def _swa_kernel(sink_ref, q_ref, kp_ref, kc_ref, vp_ref, vc_ref, bias_ref, o_ref):
    i = pl.program_id(1)
    tq = TQ_BAND
    kk = jnp.concatenate([kp_ref[0], kc_ref[0]], axis=0)
    vv = jnp.concatenate([vp_ref[0], vc_ref[0]], axis=0)
    lane = lax.broadcasted_iota(jnp.int32, (tq, LANES), 1)
    col = lax.broadcasted_iota(jnp.int32, (tq, 2 * tq), 1)
    key_ok = jnp.logical_or(col >= tq, i > 0)
    for m in range(SWA_HEADS // 2):
        qq = q_ref[0, :, m * LANES:(m + 1) * LANES]
        zero = jnp.zeros_like(qq)
        outs = []
        for hh in range(2):
            h = m + hh * SWA_GROUP
            qx = jnp.where(lane < HALF_LANES, qq, zero) if hh == 0 else jnp.where(lane < HALF_LANES, zero, qq)
            s = _dot_nt(qx, kk) + bias_ref[h]
            s = jnp.where(key_ok, s, NEG)
            sink = sink_ref[h]
            mx = jnp.maximum(jnp.max(s, axis=-1, keepdims=True), sink)
            p = jnp.exp(s - mx)
            den = jnp.sum(p, axis=-1, keepdims=True) + jnp.exp(sink - mx)
            o = jnp.dot(p.astype(BF16), vv, preferred_element_type=F32)
            outs.append(o / den)
        o_ref[0, :, m * LANES:(m + 1) * LANES] = jnp.where(lane < HALF_LANES, outs[0], outs[1]).astype(BF16)


def _swa_call(sinks, q, k, v, bias):
    B, S, _ = q.shape
    tq = TQ_BAND
    prev = lambda b, i: (b, jnp.maximum(i - 1, 0), 0)
    cur = lambda b, i: (b, i, 0)
    return pl.pallas_call(
        _swa_kernel,
        grid=(B, S // tq),
        in_specs=[pl.BlockSpec(memory_space=pltpu.SMEM),
                  pl.BlockSpec((1, tq, q.shape[2]), cur),
                  pl.BlockSpec((1, tq, LANES), prev), pl.BlockSpec((1, tq, LANES), cur),
                  pl.BlockSpec((1, tq, LANES), prev), pl.BlockSpec((1, tq, LANES), cur),
                  pl.BlockSpec(bias.shape, lambda b, i: (0, 0, 0))],
        out_specs=pl.BlockSpec((1, tq, q.shape[2]), cur),
        out_shape=jax.ShapeDtypeStruct(q.shape, BF16),
        compiler_params=_cparams(("parallel", "arbitrary")),
        name="sliding_window_attention",
    )(sinks, q, k, k, v, v, bias)


CA_KBLOCKS = CA_LEFT_CHUNKS * CHUNK // TQ_CA + 1


def _ca_kernel(q_ref, k_ref, v_ref, bias_ref, o_ref):
    i = pl.program_id(1)
    tq = TQ_CA
    nk = CA_KBLOCKS * tq
    lane = lax.broadcasted_iota(jnp.int32, (tq, LANES), 1)
    col = lax.broadcasted_iota(jnp.int32, (tq, nk), 1)
    key_ok = (col // tq) >= (CA_KBLOCKS - 1 - i)
    starts = [pl.multiple_of(jnp.maximum(i - (CA_KBLOCKS - 1) + j, 0) * tq, tq) for j in range(CA_KBLOCKS)]

    def scores(h):
        sl = slice((h // 2) * LANES, (h // 2 + 1) * LANES)
        kk = jnp.concatenate([k_ref[0, pl.ds(st, tq), sl] for st in starts], axis=0)
        qq = q_ref[0, :, sl]
        zero = jnp.zeros_like(qq)
        qx = jnp.where(lane < HALF_LANES, qq, zero) if h % 2 == 0 else jnp.where(lane < HALF_LANES, zero, qq)
        return _dot_nt(qx, kk)

    s_next = scores(0)
    outs = []
    for h in range(CA_HEADS):
        sl = slice((h // 2) * LANES, (h // 2 + 1) * LANES)
        s = jnp.where(key_ok, s_next + bias_ref[h], NEG)
        if h + 1 < CA_HEADS:
            s_next = scores(h + 1)
        vv = jnp.concatenate([v_ref[0, pl.ds(st, tq), sl] for st in starts], axis=0)
        mx = jnp.max(s, axis=-1, keepdims=True)
        p = jnp.exp(s - mx)
        den = jnp.sum(p, axis=-1, keepdims=True)
        o = jnp.dot(p.astype(BF16), vv, preferred_element_type=F32)
        outs.append(o / den)
        if h % 2 == 1:
            o_ref[0, :, sl] = jnp.where(lane < HALF_LANES, outs[0], outs[1]).astype(BF16)
            outs = []


def _ca_call(q, k, v, bias):
    B, S, W = q.shape
    tq = TQ_CA
    return pl.pallas_call(
        _ca_kernel,
        grid=(B, S // tq),
        in_specs=[pl.BlockSpec((1, tq, W), lambda b, i: (b, i, 0)),
                  pl.BlockSpec((1, S, W), lambda b, i: (b, 0, 0)),
                  pl.BlockSpec((1, S, W), lambda b, i: (b, 0, 0)),
                  pl.BlockSpec(bias.shape, lambda b, i: (0, 0, 0))],
        out_specs=pl.BlockSpec((1, tq, W), lambda b, i: (b, i, 0)),
        out_shape=jax.ShapeDtypeStruct(q.shape, BF16),
        compiler_params=_cparams(("parallel", "arbitrary")),
        name="chunk_attention",
    )(q, k, v, bias)


def _in1_kernel(x_ref, gmix_ref, win_ref, gq_ref, gk_ref, q_ref, k_ref, v_ref):
    xn = _rms(x_ref[...], gmix_ref[...]).astype(BF16)
    p = jnp.dot(xn, win_ref[...], preferred_element_type=F32)
    w = CA_HEADS * CA_HEAD_DIM
    for m in range(CA_HEADS // 2):
        sl = slice(m * LANES, (m + 1) * LANES)
        q_ref[:, sl] = _rms_two_heads(p[:, m * LANES:(m + 1) * LANES], gq_ref[...], CA_HEAD_DIM ** -0.5).astype(BF16)
        k_ref[:, sl] = _rms_two_heads(p[:, w + m * LANES:w + (m + 1) * LANES], gk_ref[...], 1.0).astype(BF16)
    v_ref[...] = p[:, 2 * w:3 * w].astype(BF16)


def _in1_call(x2d, gmix, win, gq, gk):
    T = x2d.shape[0]
    tm = TM_PROJ
    w = CA_HEADS * CA_HEAD_DIM
    row = lambda i: (i, 0)
    const = lambda i: (0, 0)
    full = lambda a: pl.BlockSpec(a.shape, const)
    out_shapes = tuple(jax.ShapeDtypeStruct((T, w), BF16) for _ in range(3))
    return pl.pallas_call(
        _in1_kernel,
        grid=(T // tm,),
        in_specs=[pl.BlockSpec((tm, D_MODEL), row), full(gmix), full(win), full(gq), full(gk)],
        out_specs=tuple(pl.BlockSpec((tm, w), row) for _ in range(3)),
        out_shape=out_shapes,
        compiler_params=_cparams(("parallel",)),
        name="layer1_in_proj",
    )(x2d, gmix, win, gq, gk)


def _out_kernel(x_ref, o_ref, w_ref, g_ref, xo_ref, xnt_ref):
    y = x_ref[...] + jnp.dot(o_ref[...], w_ref[...], preferred_element_type=F32)
    xo_ref[...] = y
    xnt_ref[...] = _rms(y, g_ref[...]).T.astype(BF16)


def _out_call(x2d, o2d, w, g):
    T = x2d.shape[0]
    tm = TM_PROJ
    row = lambda i: (i, 0)
    const = lambda i: (0, 0)
    return pl.pallas_call(
        _out_kernel,
        grid=(T // tm,),
        in_specs=[pl.BlockSpec((tm, D_MODEL), row), pl.BlockSpec((tm, o2d.shape[1]), row),
                  pl.BlockSpec(w.shape, const), pl.BlockSpec(g.shape, const)],
        out_specs=(pl.BlockSpec((tm, D_MODEL), row), pl.BlockSpec((D_MODEL, tm), lambda i: (0, i))),
        out_shape=(jax.ShapeDtypeStruct((T, D_MODEL), F32), jax.ShapeDtypeStruct((D_MODEL, T), BF16)),
        compiler_params=_cparams(("parallel",)),
        name="out_proj_residual",
    )(x2d, o2d, w, g)


NTB_PEER = TQ_PEER // LANES
I1_PER_STEP = EB_PEER // PEER_N_KEYS


SUBLANES = 8
BIG = 3.0e38

SORT16 = (
    (0, 13), (1, 12), (2, 15), (3, 14), (4, 8), (5, 6), (7, 11), (9, 10),
    (0, 5), (1, 7), (2, 9), (3, 4), (6, 13), (8, 14), (10, 15), (11, 12),
    (0, 1), (2, 3), (4, 5), (6, 8), (7, 9), (10, 11), (12, 13), (14, 15),
    (0, 2), (1, 3), (4, 10), (5, 11), (6, 7), (8, 9), (12, 14), (13, 15),
    (1, 2), (3, 12), (4, 6), (5, 7), (8, 10), (9, 11), (13, 14),
    (1, 4), (2, 6), (5, 8), (7, 10), (9, 13), (11, 14),
    (2, 4), (3, 6), (9, 12), (11, 13),
    (3, 5), (6, 8), (7, 9), (10, 12),
    (3, 4), (5, 6), (7, 8), (9, 10), (11, 12),
    (6, 7), (8, 9),
)
BITONIC16 = tuple((r, r + st) for st in (8, 4, 2, 1) for r in range(16) if (r // st) % 2 == 0)


def _compare_exchange(v, i, j):
    hi, lo = jnp.maximum(v[i], v[j]), jnp.minimum(v[i], v[j])
    v[i], v[j] = hi, lo


def _merge_across_sublanes(v):
    for shift in (4, 2, 1):
        y = [pltpu.roll(t, shift, 0) for t in v]
        v = [jnp.maximum(v[r], y[PEER_TOPK - 1 - r]) for r in range(PEER_TOPK)]
        for i, j in BITONIC16:
            _compare_exchange(v, i, j)
    return v


def _top16(x):
    v = [x[SUBLANES * i:SUBLANES * (i + 1), :] for i in range(PEER_N_KEYS // SUBLANES)]
    for i, j in SORT16:
        _compare_exchange(v, i, j)
    return _merge_across_sublanes(v)


def _route_tile(s1, s2):
    ta = _top16(s1)
    tb = _top16(s2)
    sub = lax.broadcasted_iota(jnp.int32, (SUBLANES, LANES), 0)
    a8 = ta[SUBLANES - 1]
    for j in range(SUBLANES - 2, -1, -1):
        a8 = jnp.where(sub == j, ta[j], a8)
    cl = []
    for k in range(PEER_TOPK):
        nj = PEER_TOPK // (k + 1)
        l = a8 + tb[k]
        cl.append(l if nj >= SUBLANES else jnp.where(sub < nj, l, NEG))
    m = _merge_across_sublanes(list(cl))
    ex = [ta[SUBLANES + r] + tb[0] for r in range(SUBLANES)]
    top = m[:SUBLANES] + [jnp.maximum(m[r], ex[PEER_TOPK - 1 - r]) for r in range(SUBLANES, PEER_TOPK)]
    for i, j in BITONIC16:
        _compare_exchange(top, i, j)
    tau = top[PEER_TOPK - 1]
    z = jnp.exp(top[0] - top[0])
    for r in range(1, PEER_TOPK):
        z = z + jnp.exp(top[r] - top[0])
    rz = 0.5 / z

    e2k = [jnp.exp(tb[k] - tb[0]) for k in range(PEER_TOPK)]
    e1top = jnp.exp(a8 - ta[0]) * rz
    thr = jnp.full((SUBLANES, LANES), BIG, F32)
    for k in range(PEER_TOPK):
        thr = jnp.minimum(thr, jnp.where(cl[k] >= tau, e1top * e2k[k], BIG))
    for r in range(SUBLANES):
        e1x = jnp.exp(ta[SUBLANES + r] - ta[0]) * rz
        thr = jnp.minimum(thr, jnp.where(ex[r] >= tau, e1x * e2k[0], BIG))
    for shift in (4, 2, 1):
        thr = jnp.minimum(thr, pltpu.roll(thr, shift, 0))
    n = PEER_N_KEYS // SUBLANES
    e1 = jnp.concatenate([jnp.exp(s1[SUBLANES * i:SUBLANES * (i + 1), :] - ta[0]) * rz for i in range(n)], axis=0)
    e2 = jnp.concatenate([jnp.exp(s2[SUBLANES * i:SUBLANES * (i + 1), :] - tb[0]) for i in range(n)], axis=0)
    return e1, e2, thr


def _peer_kernel(x_ref, xnt_ref, wqt_ref, sk_ref, u_ref, vt_ref, o_ref,
                 q_scr, sc_scr, e1_scr, e2_scr, thr_scr, a_scr, g_scr, acc_scr):
    e = pl.program_id(1)

    @pl.when(e == 0)
    def _route():
        q_scr[...] = jnp.dot(wqt_ref[...], xnt_ref[...], preferred_element_type=F32).astype(BF16)
        acc_scr[...] = jnp.zeros_like(acc_scr)

        def per_head(h, _):
            for p in range(2):
                hp = h * 2 + p
                qs = q_scr[pl.ds(pl.multiple_of(hp * PEER_HALF, PEER_HALF), PEER_HALF), :]
                sc = jnp.dot(sk_ref[hp], qs, preferred_element_type=F32)
                for tb in range(NTB_PEER):
                    sc_scr[p, tb] = sc[:, tb * LANES:(tb + 1) * LANES]

            def per_tile(tb, _):
                e1, e2, thr = _route_tile(sc_scr[0, tb], sc_scr[1, tb])
                e1_scr[h, tb] = e1
                e2_scr[h, tb] = e2
                thr_scr[h, tb] = thr
                return 0

            lax.fori_loop(0, NTB_PEER, per_tile, 0)
            return 0

        lax.fori_loop(0, PEER_HEADS, per_head, 0)

    n_sub = TQ_PEER // TS_PEER

    def route_weights(s, key0):
        for tb2 in range(TS_PEER // LANES):
            tb = s * (TS_PEER // LANES) + tb2
            ts = slice(tb2 * LANES, (tb2 + 1) * LANES)
            for q in range(PEER_N_KEYS // QR_PEER):
                wsum = [jnp.zeros((QR_PEER, LANES), F32) for _ in range(KEY_GROUP)]
                for h in range(PEER_HEADS):
                    e2q = jnp.maximum(e2_scr[h, tb, q * QR_PEER:(q + 1) * QR_PEER, :], 0.0)
                    thr = thr_scr[h, tb, 0:1, :]
                    for r in range(KEY_GROUP):
                        w = e2q * e1_scr[h, tb, pl.ds(e * I1_PER_STEP + key0 + r, 1), :]
                        wsum[r] = wsum[r] + jnp.where(w >= thr, w, 0.0)
                for r in range(KEY_GROUP):
                    row0 = (key0 + r) * PEER_N_KEYS + q * QR_PEER
                    rows = slice(row0, row0 + QR_PEER)
                    at = a_scr[s, rows, ts]
                    g = wsum[r] * at * (1.0 + jnp.tanh(at * (GELU_C0 + GELU_C1 * (at * at))))
                    g_scr[s, rows, ts] = g.astype(BF16)

    for s in range(n_sub):
        a_scr[s] = jnp.dot(u_ref[...], xnt_ref[:, s * TS_PEER:(s + 1) * TS_PEER],
                           preferred_element_type=F32)
    for s in range(n_sub):
        for key0 in range(0, I1_PER_STEP, KEY_GROUP):
            route_weights(s, key0)
        acc_scr[:, s * TS_PEER:(s + 1) * TS_PEER] += jnp.dot(vt_ref[...], g_scr[s], preferred_element_type=F32)

    @pl.when(e == pl.num_programs(1) - 1)
    def _finish():
        o_ref[...] = x_ref[...] + acc_scr[...].T


def _peer_call(x2d, xnt, wqt, sk, u, vt):
    T = x2d.shape[0]
    tq, eb = TQ_PEER, EB_PEER
    n_exp = u.shape[0]
    return pl.pallas_call(
        _peer_kernel,
        grid=(T // tq, n_exp // eb),
        in_specs=[pl.BlockSpec((tq, D_MODEL), lambda t, e: (t, 0)),
                  pl.BlockSpec((D_MODEL, tq), lambda t, e: (0, t)),
                  pl.BlockSpec(wqt.shape, lambda t, e: (0, 0)),
                  pl.BlockSpec(sk.shape, lambda t, e: (0, 0, 0)),
                  pl.BlockSpec((eb, D_MODEL), lambda t, e: (e, 0)),
                  pl.BlockSpec((D_MODEL, eb), lambda t, e: (0, e))],
        out_specs=pl.BlockSpec((tq, D_MODEL), lambda t, e: (t, 0)),
        out_shape=jax.ShapeDtypeStruct((T, D_MODEL), F32),
        scratch_shapes=[
            pltpu.VMEM((2 * PEER_HEADS * PEER_HALF, tq), BF16),
            pltpu.VMEM((2, NTB_PEER, PEER_N_KEYS, LANES), F32),
            pltpu.VMEM((PEER_HEADS, NTB_PEER, PEER_N_KEYS, LANES), F32),
            pltpu.VMEM((PEER_HEADS, NTB_PEER, PEER_N_KEYS, LANES), F32),
            pltpu.VMEM((PEER_HEADS, NTB_PEER, SUBLANES, LANES), F32),
            pltpu.VMEM((tq // TS_PEER, eb, TS_PEER), F32),
            pltpu.VMEM((tq // TS_PEER, eb, TS_PEER), BF16),
            pltpu.VMEM((D_MODEL, tq), F32),
        ],
        compiler_params=_cparams(("parallel", "arbitrary")),
        name="peer_dense",
    )(x2d, xnt, wqt, sk, u, vt)


def _t5_bucket(rel):
    nb = T5_BUCKETS // 2
    max_exact = nb // 2
    ret = jnp.where(rel > 0, nb, 0)
    n = jnp.abs(rel)
    large = max_exact + (jnp.log(jnp.maximum(n, 1).astype(F32) / max_exact)
                         / math.log(T5_MAX_DIST / max_exact) * (nb - max_exact)).astype(jnp.int32)
    large = jnp.minimum(large, nb - 1)
    return ret + jnp.where(n < max_exact, n, large)


def _toeplitz(f, n_rows, n_cols):
    heads, width = f.shape
    period = jnp.concatenate([f[:, n_rows - 1:], jnp.zeros((heads, 1), f.dtype), f[:, :n_rows - 1]], axis=1)
    stream = jnp.tile(period, (1, n_rows))[:, :n_rows * width]
    return stream.reshape(heads, n_rows, width)[:, :, :n_cols]


def _swa_bias(t5_table):
    tq = TQ_BAND
    r = jnp.arange(tq)[:, None]
    c = jnp.arange(2 * tq)[None, :]
    rel_line = jnp.arange(3 * tq - 1) - (2 * tq - 1)
    bias = _toeplitz(t5_table[_t5_bucket(rel_line)].astype(F32).T, tq, 2 * tq)
    qc = (tq + r) // CHUNK
    kc = c // CHUNK
    vis = jnp.logical_and(kc >= qc - SWA_BAND, kc <= qc)
    return jnp.where(vis[None], bias, NEG)


def _ca_bias(rel_table):
    tq = TQ_CA
    left = CA_LEFT_CHUNKS * CHUNK
    r = jnp.arange(tq)[:, None]
    c = jnp.arange(left + tq)[None, :]
    rel_line = jnp.arange(left + 2 * tq - 1) - (left + tq - 1)
    idx = jnp.clip(rel_line, -CA_REL_PAST, CHUNK - 1) + CA_REL_PAST
    bias = _toeplitz(rel_table[:, idx].astype(F32), tq, left + tq)
    qc = (left + r) // CHUNK
    kc = c // CHUNK
    vis = jnp.logical_and(kc >= qc - CA_LEFT_CHUNKS, kc <= qc)
    return jnp.where(vis[None], bias, NEG)


def _rope_tables(seq):
    half = MLA_ROPE // 2
    inv = ROPE_THETA ** (-jnp.arange(0, MLA_ROPE, 2, dtype=F32) / MLA_ROPE)
    ang = jnp.arange(seq, dtype=F32)[:, None] * inv[None, :]
    cos, sin = jnp.cos(ang), jnp.sin(ang)
    z = lambda n: jnp.zeros((seq, n), F32)
    tail = LANES - MLA_QK
    rc = jnp.concatenate([jnp.ones((seq, MLA_NOPE), F32), cos, cos, z(tail)], axis=1)
    rs1 = jnp.concatenate([z(MLA_NOPE), -sin, z(half), z(tail)], axis=1)
    rs2 = jnp.concatenate([z(MLA_NOPE), z(half), sin, z(tail)], axis=1)
    return rc, rs1, rs2


SWA_HEAD_ORDER = tuple(h for m in range(SWA_GROUP) for h in (m, m + SWA_GROUP))


def _layer0_weights(ev_w_in, ev_w_out, mla_w_uq, mla_w_ukv, mla_g_q, mla_g_k, swa_g_q, swa_g_k):
    pad_head = LANES - MLA_QK
    mla_in = MLA_Q_RANK + MLA_KV_RANK + MLA_ROPE
    nq = SWA_HEADS * SWA_HEAD_DIM
    nkv = SWA_KV_HEADS * SWA_HEAD_DIM
    order = jnp.asarray(SWA_HEAD_ORDER)
    w_cq_ckv = ev_w_in[:, :MLA_Q_RANK + MLA_KV_RANK]
    w_kpe = ev_w_in[:, MLA_Q_RANK + MLA_KV_RANK:mla_in]
    w_sq = ev_w_in[:, mla_in:mla_in + nq].reshape(D_MODEL, SWA_HEADS, SWA_HEAD_DIM)[:, order].reshape(D_MODEL, nq)
    w_skv = ev_w_in[:, mla_in + nq:mla_in + nq + 2 * nkv]
    win = jnp.concatenate([w_cq_ckv, w_sq, w_skv, w_kpe, jnp.zeros((D_MODEL, LANES - MLA_ROPE), F32)], axis=1)
    wq = jnp.pad(mla_w_uq, ((0, 0), (0, 0), (0, pad_head))).reshape(MLA_Q_RANK, MLA_HEADS * LANES)
    wkc = jnp.pad(mla_w_ukv[:, :, :MLA_NOPE], ((0, 0), (0, 0), (0, LANES - MLA_NOPE))).reshape(MLA_KV_RANK, MLA_HEADS * LANES)
    place = jnp.pad(jnp.eye(MLA_ROPE, dtype=F32), ((0, LANES - MLA_ROPE), (MLA_NOPE, pad_head)))
    wkp = jnp.tile(place, (1, MLA_HEADS))
    wv = mla_w_ukv[:, :, MLA_NOPE:].reshape(MLA_KV_RANK, MLA_HEADS * MLA_V)
    gq = jnp.pad(mla_g_q, (0, pad_head)).reshape(1, LANES)
    gk = jnp.pad(mla_g_k, (0, pad_head)).reshape(1, LANES)
    sgq = jnp.tile(swa_g_q, 2).reshape(1, LANES)
    sgk = jnp.tile(swa_g_k, 2).reshape(1, LANES)
    n_a = MLA_HEADS * MLA_V
    w_out_b = ev_w_out[n_a:].reshape(SWA_HEADS, SWA_HEAD_DIM, D_MODEL)[order].reshape(nq, D_MODEL)
    wout = jnp.concatenate([ev_w_out[:n_a], w_out_b], axis=0)
    return dict(win=win.astype(BF16), wq=wq.astype(BF16), wkc=wkc.astype(BF16), wkp=wkp.astype(BF16),
                wv=wv.astype(BF16), gq=gq, gk=gk, sgq=sgq, sgk=sgk, wout=wout.astype(BF16))


def _peer_layer(x2d, xnt, w_query, sub_keys, u_emb, v_emb):
    wqt = w_query.reshape(D_MODEL, 2 * PEER_HEADS * PEER_HALF).T.astype(BF16)
    sk = sub_keys.reshape(2 * PEER_HEADS, PEER_N_KEYS, PEER_HALF).astype(BF16)
    return _peer_call(x2d, xnt, wqt, sk, u_emb.astype(BF16), v_emb.T.astype(BF16))


def kernel(x, t5_bias, norm_mix, norm_ffn, ev_w_in, ev_w_out, mla_g_cq, mla_w_uq, mla_g_ckv, mla_w_ukv,
           mla_g_q, mla_g_k, swa_g_q, swa_g_k, swa_sinks, od_w_in, od_w_out, ca_g_q, ca_g_k, ca_rel_bias,
           peer_w_query, peer_sub_keys, peer_u, peer_v):
    B, S, D = x.shape
    T = B * S
    x2d = x.reshape(T, D)
    row = lambda a: a.reshape(1, -1)

    w0 = _layer0_weights(ev_w_in[0], ev_w_out[0], mla_w_uq[0], mla_w_ukv[0], mla_g_q[0], mla_g_k[0],
                         swa_g_q[0], swa_g_k[0])
    rc, rs1, rs2 = _rope_tables(S)
    qm, km, vm, qs, ks, vs = _in0_call(
        x2d, S, row(norm_mix[0]), w0["win"], row(mla_g_cq[0]), w0["wq"], w0["gq"], row(mla_g_ckv[0]),
        w0["wkc"], w0["wkp"], w0["gk"], w0["wv"], w0["sgq"], w0["sgk"], rc, rs1, rs2)
    b3 = lambda a: a.reshape(B, S, a.shape[1])
    o_a = _mla_call(b3(qm), b3(km), b3(vm))
    o_b = _swa_call(swa_sinks[0], b3(qs), b3(ks), b3(vs), _swa_bias(t5_bias))
    o0 = jnp.concatenate([o_a, o_b], axis=-1).reshape(T, -1)
    x2d, xnt = _out_call(x2d, o0, w0["wout"], row(norm_ffn[0]))
    x2d = _peer_layer(x2d, xnt, peer_w_query[0], peer_sub_keys[0], peer_u[0], peer_v[0])

    gq2 = jnp.tile(ca_g_q[0], 2).reshape(1, LANES)
    gk2 = jnp.tile(ca_g_k[0], 2).reshape(1, LANES)
    qc, kc, vc = _in1_call(x2d, row(norm_mix[1]), od_w_in[0].astype(BF16), gq2, gk2)
    o_c = _ca_call(b3(qc), b3(kc), b3(vc), _ca_bias(ca_rel_bias[0]))
    x2d, xnt = _out_call(x2d, o_c.reshape(T, -1), od_w_out[0].astype(BF16), row(norm_ffn[1]))
    x2d = _peer_layer(x2d, xnt, peer_w_query[1], peer_sub_keys[1], peer_u[1], peer_v[1])
    return x2d.reshape(B, S, D)
```

```python
import math

import jax
import jax.numpy as jnp
from jax import lax
from jax.experimental import pallas as pl
from jax.experimental.pallas import tpu as pltpu

F32 = jnp.float32
BF16 = jnp.bfloat16

D_MODEL = 1024
CHUNK = 64
EPS = 1e-6
NEG = -1e30
LANES = 128
HALF_LANES = LANES // 2

T5_BUCKETS = 32
T5_MAX_DIST = 128

MLA_HEADS = 8
MLA_NOPE = 64
MLA_ROPE = 32
MLA_V = 64
MLA_QK = MLA_NOPE + MLA_ROPE
MLA_Q_RANK = 384
MLA_KV_RANK = 256
ROPE_THETA = 10000.0

SWA_HEADS = 8
SWA_KV_HEADS = 2
SWA_GROUP = SWA_HEADS // SWA_KV_HEADS
SWA_HEAD_DIM = 64
SWA_BAND = 2

CA_HEADS = 16
CA_HEAD_DIM = 64
CA_LEFT_CHUNKS = 8
CA_REL_PAST = 256

PEER_HEADS = 8
PEER_N_KEYS = 128
PEER_HALF = 128
PEER_TOPK = 16

VMEM_LIMIT = 56 * 1024 * 1024

TM_IN0 = 256
TM_PROJ = 512
TQ_MLA = 512
TQ_BAND = 128
TQ_CA = 256
TQ_PEER = 512
EB_PEER = 2048
TS_PEER = 256
QR_PEER = 16
KEY_GROUP = 8

GELU_C0 = math.sqrt(2.0 / math.pi)
GELU_C1 = GELU_C0 * 0.044715


def _cparams(sem):
    return pltpu.CompilerParams(dimension_semantics=sem, vmem_limit_bytes=VMEM_LIMIT)


def _rms(x, g):
    ms = jnp.mean(x * x, axis=-1, keepdims=True)
    return x * lax.rsqrt(ms + EPS) * g


def _rms_two_heads(x, g2, scale):
    lane = lax.broadcasted_iota(jnp.int32, x.shape, 1)
    lo_mask = lane < HALF_LANES
    s = x * x
    lo = jnp.sum(jnp.where(lo_mask, s, 0.0), axis=-1, keepdims=True)
    hi = jnp.sum(jnp.where(lo_mask, 0.0, s), axis=-1, keepdims=True)
    r = jnp.where(lo_mask, lax.rsqrt(lo * (1.0 / HALF_LANES) + EPS), lax.rsqrt(hi * (1.0 / HALF_LANES) + EPS))
    return x * r * (g2 * scale)


def _dot_nt(a, b):
    return lax.dot_general(a, b, (((1,), (1,)), ((), ())), preferred_element_type=F32)


COL_CQ = 0
COL_CKV = COL_CQ + MLA_Q_RANK
COL_SQ = COL_CKV + MLA_KV_RANK
COL_SK = COL_SQ + SWA_HEADS * SWA_HEAD_DIM
COL_SV = COL_SK + SWA_KV_HEADS * SWA_HEAD_DIM
COL_KPE = COL_SV + SWA_KV_HEADS * SWA_HEAD_DIM
COL_END = COL_KPE + LANES
def _in0_kernel(x_ref, gmix_ref, win_ref, gcq_ref, wq_ref, gq_ref, gckv_ref, wkc_ref, wkp_ref, gk_ref,
                wv_ref, sgq_ref, sgk_ref, rc_ref, rs1_ref, rs2_ref,
                qm_ref, km_ref, vm_ref, qs_ref, ks_ref, vs_ref):
    xn = _rms(x_ref[...], gmix_ref[...]).astype(BF16)
    p = jnp.dot(xn, win_ref[...], preferred_element_type=F32)
    cq = _rms(p[:, COL_CQ:COL_CKV], gcq_ref[...]).astype(BF16)
    ckv = _rms(p[:, COL_CKV:COL_SQ], gckv_ref[...]).astype(BF16)
    kpe = p[:, COL_KPE:COL_END].astype(BF16)
    q = jnp.dot(cq, wq_ref[...], preferred_element_type=F32)
    k = (jnp.dot(ckv, wkc_ref[...], preferred_element_type=F32)
         + jnp.dot(kpe, wkp_ref[...], preferred_element_type=F32))
    vm_ref[...] = jnp.dot(ckv, wv_ref[...], preferred_element_type=F32).astype(BF16)

    rc, rs1, rs2 = rc_ref[...], rs1_ref[...], rs2_ref[...]

    def head_norm_rope(t, g, scale):
        ms = jnp.sum(t * t, axis=-1, keepdims=True) * (1.0 / MLA_QK)
        tn = t * lax.rsqrt(ms + EPS) * g
        tr = tn * rc + pltpu.roll(tn, LANES - MLA_ROPE // 2, 1) * rs1 + pltpu.roll(tn, MLA_ROPE // 2, 1) * rs2
        return (tr * scale).astype(BF16)

    for h in range(MLA_HEADS):
        sl = slice(h * LANES, (h + 1) * LANES)
        qm_ref[:, sl] = head_norm_rope(q[:, sl], gq_ref[...], MLA_QK ** -0.5)
        km_ref[:, sl] = head_norm_rope(k[:, sl], gk_ref[...], 1.0)

    for m in range(SWA_HEADS // 2):
        sl = slice(COL_SQ + m * LANES, COL_SQ + (m + 1) * LANES)
        qs_ref[:, m * LANES:(m + 1) * LANES] = _rms_two_heads(p[:, sl], sgq_ref[...], SWA_HEAD_DIM ** -0.5).astype(BF16)
    ks_ref[...] = _rms_two_heads(p[:, COL_SK:COL_SV], sgk_ref[...], 1.0).astype(BF16)
    vs_ref[...] = p[:, COL_SV:COL_KPE].astype(BF16)


def _in0_call(x2d, seq, gmix, win, gcq, wq, gq, gckv, wkc, wkp, gk, wv, sgq, sgk, rc, rs1, rs2):
    T = x2d.shape[0]
    tm = TM_IN0
    nblk_seq = seq // tm
    row = lambda i: (i, 0)
    const = lambda i: (0, 0)
    pos = lambda i: (i % nblk_seq, 0)
    full = lambda a: pl.BlockSpec(a.shape, const)
    out_shapes = (
        jax.ShapeDtypeStruct((T, MLA_HEADS * LANES), BF16),
        jax.ShapeDtypeStruct((T, MLA_HEADS * LANES), BF16),
        jax.ShapeDtypeStruct((T, MLA_HEADS * MLA_V), BF16),
        jax.ShapeDtypeStruct((T, SWA_HEADS * SWA_HEAD_DIM), BF16),
        jax.ShapeDtypeStruct((T, SWA_KV_HEADS * SWA_HEAD_DIM), BF16),
        jax.ShapeDtypeStruct((T, SWA_KV_HEADS * SWA_HEAD_DIM), BF16),
    )
    return pl.pallas_call(
        _in0_kernel,
        grid=(T // tm,),
        in_specs=[pl.BlockSpec((tm, D_MODEL), row), full(gmix), full(win), full(gcq), full(wq), full(gq),
                  full(gckv), full(wkc), full(wkp), full(gk), full(wv), full(sgq), full(sgk),
                  pl.BlockSpec((tm, LANES), pos), pl.BlockSpec((tm, LANES), pos), pl.BlockSpec((tm, LANES), pos)],
        out_specs=tuple(pl.BlockSpec((tm, s.shape[1]), row) for s in out_shapes),
        out_shape=out_shapes,
        compiler_params=_cparams(("parallel",)),
        name="layer0_in_proj",
    )(x2d, gmix, win, gcq, wq, gq, gckv, wkc, wkp, gk, wv, sgq, sgk, rc, rs1, rs2)


def _mla_kernel(q_ref, k_ref, v_ref, o_ref, m_scr, l_scr, acc_scr):
    i = pl.program_id(1)
    tq = TQ_MLA
    rows = lax.broadcasted_iota(jnp.int32, (tq, tq), 0)
    cols = lax.broadcasted_iota(jnp.int32, (tq, tq), 1)
    diag_ok = (cols // CHUNK) <= (rows // CHUNK)
    lane = lax.broadcasted_iota(jnp.int32, (tq, LANES), 1)
    m_scr[...] = jnp.full(m_scr.shape, NEG, F32)
    l_scr[...] = jnp.zeros_like(l_scr)
    acc_scr[...] = jnp.zeros_like(acc_scr)

    def body(j, _):
        start = pl.multiple_of(j * tq, tq)
        ok = jnp.logical_or(j < i, diag_ok)

        def scores(h):
            return _dot_nt(q_ref[0, :, h * LANES:(h + 1) * LANES],
                           k_ref[0, pl.ds(start, tq), h * LANES:(h + 1) * LANES])

        s_next = scores(0)
        for h in range(MLA_HEADS):
            s = jnp.where(ok, s_next, NEG)
            if h + 1 < MLA_HEADS:
                s_next = scores(h + 1)
            vb = v_ref[0, pl.ds(start, tq), (h // 2) * LANES:(h // 2 + 1) * LANES]
            m_old = m_scr[h]
            m_new = jnp.maximum(m_old, jnp.max(s, axis=-1, keepdims=True))
            alpha = jnp.exp(m_old - m_new)
            p = jnp.exp(s - m_new)
            l_scr[h] = alpha * l_scr[h] + jnp.sum(p, axis=-1, keepdims=True)
            acc_scr[h] = alpha * acc_scr[h] + jnp.dot(p.astype(BF16), vb, preferred_element_type=F32)
            m_scr[h] = m_new
        return 0

    lax.fori_loop(0, i + 1, body, 0)
    for hp in range(MLA_HEADS // 2):
        lo = acc_scr[2 * hp] / l_scr[2 * hp]
        hi = acc_scr[2 * hp + 1] / l_scr[2 * hp + 1]
        o_ref[0, :, hp * LANES:(hp + 1) * LANES] = jnp.where(lane < HALF_LANES, lo, hi).astype(BF16)


def _mla_call(q, k, v):
    B, S, _ = q.shape
    tq = TQ_MLA
    return pl.pallas_call(
        _mla_kernel,
        grid=(B, S // tq),
        in_specs=[pl.BlockSpec((1, tq, q.shape[2]), lambda b, i: (b, i, 0)),
                  pl.BlockSpec((1, S, k.shape[2]), lambda b, i: (b, 0, 0)),
                  pl.BlockSpec((1, S, v.shape[2]), lambda b, i: (b, 0, 0))],
        out_specs=pl.BlockSpec((1, tq, v.shape[2]), lambda b, i: (b, i, 0)),
        out_shape=jax.ShapeDtypeStruct((B, S, v.shape[2]), BF16),
        scratch_shapes=[pltpu.VMEM((MLA_HEADS, tq, 1), F32), pltpu.VMEM((MLA_HEADS, tq, 1), F32),
                        pltpu.VMEM((MLA_HEADS, tq, LANES), F32)],
        compiler_params=_cparams(("parallel", "arbitrary")),
        name="latent_attention",
    )(q, k, v)


def _swa_kernel(sink_ref, q_ref, kp_ref, kc_ref, vp_ref, vc_ref, bias_ref, o_ref):
    i = pl.program_id(1)
    tq = TQ_BAND
    kk = jnp.concatenate([kp_ref[0], kc_ref[0]], axis=0)
    vv = jnp.concatenate([vp_ref[0], vc_ref[0]], axis=0)
    lane = lax.broadcasted_iota(jnp.int32, (tq, LANES), 1)
    col = lax.broadcasted_iota(jnp.int32, (tq, 2 * tq), 1)
    key_ok = jnp.logical_or(col >= tq, i > 0)
    for m in range(SWA_HEADS // 2):
        qq = q_ref[0, :, m * LANES:(m + 1) * LANES]
        zero = jnp.zeros_like(qq)
        outs = []
        for hh in range(2):
            h = m + hh * SWA_GROUP
            qx = jnp.where(lane < HALF_LANES, qq, zero) if hh == 0 else jnp.where(lane < HALF_LANES, zero, qq)
            s = _dot_nt(qx, kk) + bias_ref[h]
            s = jnp.where(key_ok, s, NEG)
            sink = sink_ref[h]
            mx = jnp.maximum(jnp.max(s, axis=-1, keepdims=True), sink)
            p = jnp.exp(s - mx)
            den = jnp.sum(p, axis=-1, keepdims=True) + jnp.exp(sink - mx)
            o = jnp.dot(p.astype(BF16), vv, preferred_element_type=F32)
            outs.append(o / den)
        o_ref[0, :, m * LANES:(m + 1) * LANES] = jnp.where(lane < HALF_LANES, outs[0], outs[1]).astype(BF16)


def _swa_call(sinks, q, k, v, bias):
    B, S, _ = q.shape
    tq = TQ_BAND
    prev = lambda b, i: (b, jnp.maximum(i - 1, 0), 0)
    cur = lambda b, i: (b, i, 0)
    return pl.pallas_call(
        _swa_kernel,
        grid=(B, S // tq),
        in_specs=[pl.BlockSpec(memory_space=pltpu.SMEM),
                  pl.BlockSpec((1, tq, q.shape[2]), cur),
                  pl.BlockSpec((1, tq, LANES), prev), pl.BlockSpec((1, tq, LANES), cur),
                  pl.BlockSpec((1, tq, LANES), prev), pl.BlockSpec((1, tq, LANES), cur),
                  pl.BlockSpec(bias.shape, lambda b, i: (0, 0, 0))],
        out_specs=pl.BlockSpec((1, tq, q.shape[2]), cur),
        out_shape=jax.ShapeDtypeStruct(q.shape, BF16),
        compiler_params=_cparams(("parallel", "arbitrary")),
        name="sliding_window_attention",
    )(sinks, q, k, k, v, v, bias)


CA_KBLOCKS = CA_LEFT_CHUNKS * CHUNK // TQ_CA + 1


def _ca_kernel(q_ref, k_ref, v_ref, bias_ref, o_ref):
    i = pl.program_id(1)
    tq = TQ_CA
    nk = CA_KBLOCKS * tq
    lane = lax.broadcasted_iota(jnp.int32, (tq, LANES), 1)
    col = lax.broadcasted_iota(jnp.int32, (tq, nk), 1)
    key_ok = (col // tq) >= (CA_KBLOCKS - 1 - i)
    starts = [pl.multiple_of(jnp.maximum(i - (CA_KBLOCKS - 1) + j, 0) * tq, tq) for j in range(CA_KBLOCKS)]

    def scores(h):
        sl = slice((h // 2) * LANES, (h // 2 + 1) * LANES)
        kk = jnp.concatenate([k_ref[0, pl.ds(st, tq), sl] for st in starts], axis=0)
        qq = q_ref[0, :, sl]
        zero = jnp.zeros_like(qq)
        qx = jnp.where(lane < HALF_LANES, qq, zero) if h % 2 == 0 else jnp.where(lane < HALF_LANES, zero, qq)
        return _dot_nt(qx, kk)

    s_next = scores(0)
    outs = []
    for h in range(CA_HEADS):
        sl = slice((h // 2) * LANES, (h // 2 + 1) * LANES)
        s = jnp.where(key_ok, s_next + bias_ref[h], NEG)
        if h + 1 < CA_HEADS:
            s_next = scores(h + 1)
        vv = jnp.concatenate([v_ref[0, pl.ds(st, tq), sl] for st in starts], axis=0)
        mx = jnp.max(s, axis=-1, keepdims=True)
        p = jnp.exp(s - mx)
        den = jnp.sum(p, axis=-1, keepdims=True)
        o = jnp.dot(p.astype(BF16), vv, preferred_element_type=F32)
        outs.append(o / den)
        if h % 2 == 1:
            o_ref[0, :, sl] = jnp.where(lane < HALF_LANES, outs[0], outs[1]).astype(BF16)
            outs = []


def _ca_call(q, k, v, bias):
    B, S, W = q.shape
    tq = TQ_CA
    return pl.pallas_call(
        _ca_kernel,
        grid=(B, S // tq),
        in_specs=[pl.BlockSpec((1, tq, W), lambda b, i: (b, i, 0)),
                  pl.BlockSpec((1, S, W), lambda b, i: (b, 0, 0)),
                  pl.BlockSpec((1, S, W), lambda b, i: (b, 0, 0)),
                  pl.BlockSpec(bias.shape, lambda b, i: (0, 0, 0))],
        out_specs=pl.BlockSpec((1, tq, W), lambda b, i: (b, i, 0)),
        out_shape=jax.ShapeDtypeStruct(q.shape, BF16),
        compiler_params=_cparams(("parallel", "arbitrary")),
        name="chunk_attention",
    )(q, k, v, bias)


def _in1_kernel(x_ref, gmix_ref, win_ref, gq_ref, gk_ref, q_ref, k_ref, v_ref):
    xn = _rms(x_ref[...], gmix_ref[...]).astype(BF16)
    p = jnp.dot(xn, win_ref[...], preferred_element_type=F32)
    w = CA_HEADS * CA_HEAD_DIM
    for m in range(CA_HEADS // 2):
        sl = slice(m * LANES, (m + 1) * LANES)
        q_ref[:, sl] = _rms_two_heads(p[:, m * LANES:(m + 1) * LANES], gq_ref[...], CA_HEAD_DIM ** -0.5).astype(BF16)
        k_ref[:, sl] = _rms_two_heads(p[:, w + m * LANES:w + (m + 1) * LANES], gk_ref[...], 1.0).astype(BF16)
    v_ref[...] = p[:, 2 * w:3 * w].astype(BF16)


def _in1_call(x2d, gmix, win, gq, gk):
    T = x2d.shape[0]
    tm = TM_PROJ
    w = CA_HEADS * CA_HEAD_DIM
    row = lambda i: (i, 0)
    const = lambda i: (0, 0)
    full = lambda a: pl.BlockSpec(a.shape, const)
    out_shapes = tuple(jax.ShapeDtypeStruct((T, w), BF16) for _ in range(3))
    return pl.pallas_call(
        _in1_kernel,
        grid=(T // tm,),
        in_specs=[pl.BlockSpec((tm, D_MODEL), row), full(gmix), full(win), full(gq), full(gk)],
        out_specs=tuple(pl.BlockSpec((tm, w), row) for _ in range(3)),
        out_shape=out_shapes,
        compiler_params=_cparams(("parallel",)),
        name="layer1_in_proj",
    )(x2d, gmix, win, gq, gk)


def _out_kernel(x_ref, o_ref, w_ref, g_ref, xo_ref, xnt_ref):
    y = x_ref[...] + jnp.dot(o_ref[...], w_ref[...], preferred_element_type=F32)
    xo_ref[...] = y
    xnt_ref[...] = _rms(y, g_ref[...]).T.astype(BF16)


def _out_call(x2d, o2d, w, g):
    T = x2d.shape[0]
    tm = TM_PROJ
    row = lambda i: (i, 0)
    const = lambda i: (0, 0)
    return pl.pallas_call(
        _out_kernel,
        grid=(T // tm,),
        in_specs=[pl.BlockSpec((tm, D_MODEL), row), pl.BlockSpec((tm, o2d.shape[1]), row),
                  pl.BlockSpec(w.shape, const), pl.BlockSpec(g.shape, const)],
        out_specs=(pl.BlockSpec((tm, D_MODEL), row), pl.BlockSpec((D_MODEL, tm), lambda i: (0, i))),
        out_shape=(jax.ShapeDtypeStruct((T, D_MODEL), F32), jax.ShapeDtypeStruct((D_MODEL, T), BF16)),
        compiler_params=_cparams(("parallel",)),
        name="out_proj_residual",
    )(x2d, o2d, w, g)


NTB_PEER = TQ_PEER // LANES
I1_PER_STEP = EB_PEER // PEER_N_KEYS


SUBLANES = 8
BIG = 3.0e38

SORT16 = (
    (0, 13), (1, 12), (2, 15), (3, 14), (4, 8), (5, 6), (7, 11), (9, 10),
    (0, 5), (1, 7), (2, 9), (3, 4), (6, 13), (8, 14), (10, 15), (11, 12),
    (0, 1), (2, 3), (4, 5), (6, 8), (7, 9), (10, 11), (12, 13), (14, 15),
    (0, 2), (1, 3), (4, 10), (5, 11), (6, 7), (8, 9), (12, 14), (13, 15),
    (1, 2), (3, 12), (4, 6), (5, 7), (8, 10), (9, 11), (13, 14),
    (1, 4), (2, 6), (5, 8), (7, 10), (9, 13), (11, 14),
    (2, 4), (3, 6), (9, 12), (11, 13),
    (3, 5), (6, 8), (7, 9), (10, 12),
    (3, 4), (5, 6), (7, 8), (9, 10), (11, 12),
    (6, 7), (8, 9),
)
BITONIC16 = tuple((r, r + st) for st in (8, 4, 2, 1) for r in range(16) if (r // st) % 2 == 0)


def _compare_exchange(v, i, j):
    hi, lo = jnp.maximum(v[i], v[j]), jnp.minimum(v[i], v[j])
    v[i], v[j] = hi, lo


def _merge_across_sublanes(v):
    for shift in (4, 2, 1):
        y = [pltpu.roll(t, shift, 0) for t in v]
        v = [jnp.maximum(v[r], y[PEER_TOPK - 1 - r]) for r in range(PEER_TOPK)]
        for i, j in BITONIC16:
            _compare_exchange(v, i, j)
    return v


def _top16(x):
    v = [x[SUBLANES * i:SUBLANES * (i + 1), :] for i in range(PEER_N_KEYS // SUBLANES)]
    for i, j in SORT16:
        _compare_exchange(v, i, j)
    return _merge_across_sublanes(v)


def _route_tile(s1, s2):
    ta = _top16(s1)
    tb = _top16(s2)
    sub = lax.broadcasted_iota(jnp.int32, (SUBLANES, LANES), 0)
    a8 = ta[SUBLANES - 1]
    for j in range(SUBLANES - 2, -1, -1):
        a8 = jnp.where(sub == j, ta[j], a8)
    cl = []
    for k in range(PEER_TOPK):
        nj = PEER_TOPK // (k + 1)
        l = a8 + tb[k]
        cl.append(l if nj >= SUBLANES else jnp.where(sub < nj, l, NEG))
    m = _merge_across_sublanes(list(cl))
    ex = [ta[SUBLANES + r] + tb[0] for r in range(SUBLANES)]
    top = m[:SUBLANES] + [jnp.maximum(m[r], ex[PEER_TOPK - 1 - r]) for r in range(SUBLANES, PEER_TOPK)]
    for i, j in BITONIC16:
        _compare_exchange(top, i, j)
    tau = top[PEER_TOPK - 1]
    z = jnp.exp(top[0] - top[0])
    for r in range(1, PEER_TOPK):
        z = z + jnp.exp(top[r] - top[0])
    rz = 0.5 / z

    e2k = [jnp.exp(tb[k] - tb[0]) for k in range(PEER_TOPK)]
    e1top = jnp.exp(a8 - ta[0]) * rz
    thr = jnp.full((SUBLANES, LANES), BIG, F32)
    for k in range(PEER_TOPK):
        thr = jnp.minimum(thr, jnp.where(cl[k] >= tau, e1top * e2k[k], BIG))
    for r in range(SUBLANES):
        e1x = jnp.exp(ta[SUBLANES + r] - ta[0]) * rz
        thr = jnp.minimum(thr, jnp.where(ex[r] >= tau, e1x * e2k[0], BIG))
    for shift in (4, 2, 1):
        thr = jnp.minimum(thr, pltpu.roll(thr, shift, 0))
    n = PEER_N_KEYS // SUBLANES
    e1 = jnp.concatenate([jnp.exp(s1[SUBLANES * i:SUBLANES * (i + 1), :] - ta[0]) * rz for i in range(n)], axis=0)
    e2 = jnp.concatenate([jnp.exp(s2[SUBLANES * i:SUBLANES * (i + 1), :] - tb[0]) for i in range(n)], axis=0)
    return e1, e2, thr


def _peer_kernel(x_ref, xnt_ref, wqt_ref, sk_ref, u_ref, vt_ref, o_ref,
                 q_scr, sc_scr, e1_scr, e2_scr, thr_scr, a_scr, g_scr, acc_scr):
    e = pl.program_id(1)

    @pl.when(e == 0)
    def _route():
        q_scr[...] = jnp.dot(wqt_ref[...], xnt_ref[...], preferred_element_type=F32).astype(BF16)
        acc_scr[...] = jnp.zeros_like(acc_scr)

        def per_head(h, _):
            for p in range(2):
                hp = h * 2 + p
                qs = q_scr[pl.ds(pl.multiple_of(hp * PEER_HALF, PEER_HALF), PEER_HALF), :]
                sc = jnp.dot(sk_ref[hp], qs, preferred_element_type=F32)
                for tb in range(NTB_PEER):
                    sc_scr[p, tb] = sc[:, tb * LANES:(tb + 1) * LANES]

            def per_tile(tb, _):
                e1, e2, thr = _route_tile(sc_scr[0, tb], sc_scr[1, tb])
                e1_scr[h, tb] = e1
                e2_scr[h, tb] = e2
                thr_scr[h, tb] = thr
                return 0

            lax.fori_loop(0, NTB_PEER, per_tile, 0)
            return 0

        lax.fori_loop(0, PEER_HEADS, per_head, 0)

    n_sub = TQ_PEER // TS_PEER

    def route_weights(s, key0):
        for tb2 in range(TS_PEER // LANES):
            tb = s * (TS_PEER // LANES) + tb2
            ts = slice(tb2 * LANES, (tb2 + 1) * LANES)
            for q in range(PEER_N_KEYS // QR_PEER):
                wsum = [jnp.zeros((QR_PEER, LANES), F32) for _ in range(KEY_GROUP)]
                for h in range(PEER_HEADS):
                    e2q = jnp.maximum(e2_scr[h, tb, q * QR_PEER:(q + 1) * QR_PEER, :], 0.0)
                    thr = thr_scr[h, tb, 0:1, :]
                    for r in range(KEY_GROUP):
                        w = e2q * e1_scr[h, tb, pl.ds(e * I1_PER_STEP + key0 + r, 1), :]
                        wsum[r] = wsum[r] + jnp.where(w >= thr, w, 0.0)
                for r in range(KEY_GROUP):
                    row0 = (key0 + r) * PEER_N_KEYS + q * QR_PEER
                    rows = slice(row0, row0 + QR_PEER)
                    at = a_scr[s, rows, ts]
                    g = wsum[r] * at * (1.0 + jnp.tanh(at * (GELU_C0 + GELU_C1 * (at * at))))
                    g_scr[s, rows, ts] = g.astype(BF16)

    for s in range(n_sub):
        a_scr[s] = jnp.dot(u_ref[...], xnt_ref[:, s * TS_PEER:(s + 1) * TS_PEER],
                           preferred_element_type=F32)
    for s in range(n_sub):
        for key0 in range(0, I1_PER_STEP, KEY_GROUP):
            route_weights(s, key0)
        acc_scr[:, s * TS_PEER:(s + 1) * TS_PEER] += jnp.dot(vt_ref[...], g_scr[s], preferred_element_type=F32)

    @pl.when(e == pl.num_programs(1) - 1)
    def _finish():
        o_ref[...] = x_ref[...] + acc_scr[...].T


def _peer_call(x2d, xnt, wqt, sk, u, vt):
    T = x2d.shape[0]
    tq, eb = TQ_PEER, EB_PEER
    n_exp = u.shape[0]
    return pl.pallas_call(
        _peer_kernel,
        grid=(T // tq, n_exp // eb),
        in_specs=[pl.BlockSpec((tq, D_MODEL), lambda t, e: (t, 0)),
                  pl.BlockSpec((D_MODEL, tq), lambda t, e: (0, t)),
                  pl.BlockSpec(wqt.shape, lambda t, e: (0, 0)),
                  pl.BlockSpec(sk.shape, lambda t, e: (0, 0, 0)),
                  pl.BlockSpec((eb, D_MODEL), lambda t, e: (e, 0)),
                  pl.BlockSpec((D_MODEL, eb), lambda t, e: (0, e))],
        out_specs=pl.BlockSpec((tq, D_MODEL), lambda t, e: (t, 0)),
        out_shape=jax.ShapeDtypeStruct((T, D_MODEL), F32),
        scratch_shapes=[
            pltpu.VMEM((2 * PEER_HEADS * PEER_HALF, tq), BF16),
            pltpu.VMEM((2, NTB_PEER, PEER_N_KEYS, LANES), F32),
            pltpu.VMEM((PEER_HEADS, NTB_PEER, PEER_N_KEYS, LANES), F32),
            pltpu.VMEM((PEER_HEADS, NTB_PEER, PEER_N_KEYS, LANES), F32),
            pltpu.VMEM((PEER_HEADS, NTB_PEER, SUBLANES, LANES), F32),
            pltpu.VMEM((tq // TS_PEER, eb, TS_PEER), F32),
            pltpu.VMEM((tq // TS_PEER, eb, TS_PEER), BF16),
            pltpu.VMEM((D_MODEL, tq), F32),
        ],
        compiler_params=_cparams(("parallel", "arbitrary")),
        name="peer_dense",
    )(x2d, xnt, wqt, sk, u, vt)


def _t5_bucket(rel):
    nb = T5_BUCKETS // 2
    max_exact = nb // 2
    ret = jnp.where(rel > 0, nb, 0)
    n = jnp.abs(rel)
    large = max_exact + (jnp.log(jnp.maximum(n, 1).astype(F32) / max_exact)
                         / math.log(T5_MAX_DIST / max_exact) * (nb - max_exact)).astype(jnp.int32)
    large = jnp.minimum(large, nb - 1)
    return ret + jnp.where(n < max_exact, n, large)


def _toeplitz(f, n_rows, n_cols):
    heads, width = f.shape
    period = jnp.concatenate([f[:, n_rows - 1:], jnp.zeros((heads, 1), f.dtype), f[:, :n_rows - 1]], axis=1)
    stream = jnp.tile(period, (1, n_rows))[:, :n_rows * width]
    return stream.reshape(heads, n_rows, width)[:, :, :n_cols]


def _swa_bias(t5_table):
    tq = TQ_BAND
    r = jnp.arange(tq)[:, None]
    c = jnp.arange(2 * tq)[None, :]
    rel_line = jnp.arange(3 * tq - 1) - (2 * tq - 1)
    bias = _toeplitz(t5_table[_t5_bucket(rel_line)].astype(F32).T, tq, 2 * tq)
    qc = (tq + r) // CHUNK
    kc = c // CHUNK
    vis = jnp.logical_and(kc >= qc - SWA_BAND, kc <= qc)
    return jnp.where(vis[None], bias, NEG)


def _ca_bias(rel_table):
    tq = TQ_CA
    left = CA_LEFT_CHUNKS * CHUNK
    r = jnp.arange(tq)[:, None]
    c = jnp.arange(left + tq)[None, :]
    rel_line = jnp.arange(left + 2 * tq - 1) - (left + tq - 1)
    idx = jnp.clip(rel_line, -CA_REL_PAST, CHUNK - 1) + CA_REL_PAST
    bias = _toeplitz(rel_table[:, idx].astype(F32), tq, left + tq)
    qc = (left + r) // CHUNK
    kc = c // CHUNK
    vis = jnp.logical_and(kc >= qc - CA_LEFT_CHUNKS, kc <= qc)
    return jnp.where(vis[None], bias, NEG)


def _rope_tables(seq):
    half = MLA_ROPE // 2
    inv = ROPE_THETA ** (-jnp.arange(0, MLA_ROPE, 2, dtype=F32) / MLA_ROPE)
    ang = jnp.arange(seq, dtype=F32)[:, None] * inv[None, :]
    cos, sin = jnp.cos(ang), jnp.sin(ang)
    z = lambda n: jnp.zeros((seq, n), F32)
    tail = LANES - MLA_QK
    rc = jnp.concatenate([jnp.ones((seq, MLA_NOPE), F32), cos, cos, z(tail)], axis=1)
    rs1 = jnp.concatenate([z(MLA_NOPE), -sin, z(half), z(tail)], axis=1)
    rs2 = jnp.concatenate([z(MLA_NOPE), z(half), sin, z(tail)], axis=1)
    return rc, rs1, rs2


SWA_HEAD_ORDER = tuple(h for m in range(SWA_GROUP) for h in (m, m + SWA_GROUP))


def _layer0_weights(ev_w_in, ev_w_out, mla_w_uq, mla_w_ukv, mla_g_q, mla_g_k, swa_g_q, swa_g_k):
    pad_head = LANES - MLA_QK
    mla_in = MLA_Q_RANK + MLA_KV_RANK + MLA_ROPE
    nq = SWA_HEADS * SWA_HEAD_DIM
    nkv = SWA_KV_HEADS * SWA_HEAD_DIM
    order = jnp.asarray(SWA_HEAD_ORDER)
    w_cq_ckv = ev_w_in[:, :MLA_Q_RANK + MLA_KV_RANK]
    w_kpe = ev_w_in[:, MLA_Q_RANK + MLA_KV_RANK:mla_in]
    w_sq = ev_w_in[:, mla_in:mla_in + nq].reshape(D_MODEL, SWA_HEADS, SWA_HEAD_DIM)[:, order].reshape(D_MODEL, nq)
    w_skv = ev_w_in[:, mla_in + nq:mla_in + nq + 2 * nkv]
    win = jnp.concatenate([w_cq_ckv, w_sq, w_skv, w_kpe, jnp.zeros((D_MODEL, LANES - MLA_ROPE), F32)], axis=1)
    wq = jnp.pad(mla_w_uq, ((0, 0), (0, 0), (0, pad_head))).reshape(MLA_Q_RANK, MLA_HEADS * LANES)
    wkc = jnp.pad(mla_w_ukv[:, :, :MLA_NOPE], ((0, 0), (0, 0), (0, LANES - MLA_NOPE))).reshape(MLA_KV_RANK, MLA_HEADS * LANES)
    place = jnp.pad(jnp.eye(MLA_ROPE, dtype=F32), ((0, LANES - MLA_ROPE), (MLA_NOPE, pad_head)))
    wkp = jnp.tile(place, (1, MLA_HEADS))
    wv = mla_w_ukv[:, :, MLA_NOPE:].reshape(MLA_KV_RANK, MLA_HEADS * MLA_V)
    gq = jnp.pad(mla_g_q, (0, pad_head)).reshape(1, LANES)
    gk = jnp.pad(mla_g_k, (0, pad_head)).reshape(1, LANES)
    sgq = jnp.tile(swa_g_q, 2).reshape(1, LANES)
    sgk = jnp.tile(swa_g_k, 2).reshape(1, LANES)
    n_a = MLA_HEADS * MLA_V
    w_out_b = ev_w_out[n_a:].reshape(SWA_HEADS, SWA_HEAD_DIM, D_MODEL)[order].reshape(nq, D_MODEL)
    wout = jnp.concatenate([ev_w_out[:n_a], w_out_b], axis=0)
    return dict(win=win.astype(BF16), wq=wq.astype(BF16), wkc=wkc.astype(BF16), wkp=wkp.astype(BF16),
                wv=wv.astype(BF16), gq=gq, gk=gk, sgq=sgq, sgk=sgk, wout=wout.astype(BF16))


def _peer_layer(x2d, xnt, w_query, sub_keys, u_emb, v_emb):
    wqt = w_query.reshape(D_MODEL, 2 * PEER_HEADS * PEER_HALF).T.astype(BF16)
    sk = sub_keys.reshape(2 * PEER_HEADS, PEER_N_KEYS, PEER_HALF).astype(BF16)
    return _peer_call(x2d, xnt, wqt, sk, u_emb.astype(BF16), v_emb.T.astype(BF16))


def kernel(x, t5_bias, norm_mix, norm_ffn, ev_w_in, ev_w_out, mla_g_cq, mla_w_uq, mla_g_ckv, mla_w_ukv,
           mla_g_q, mla_g_k, swa_g_q, swa_g_k, swa_sinks, od_w_in, od_w_out, ca_g_q, ca_g_k, ca_rel_bias,
           peer_w_query, peer_sub_keys, peer_u, peer_v):
    B, S, D = x.shape
    T = B * S
    x2d = x.reshape(T, D)
    row = lambda a: a.reshape(1, -1)

    w0 = _layer0_weights(ev_w_in[0], ev_w_out[0], mla_w_uq[0], mla_w_ukv[0], mla_g_q[0], mla_g_k[0],
                         swa_g_q[0], swa_g_k[0])
    rc, rs1, rs2 = _rope_tables(S)
    qm, km, vm, qs, ks, vs = _in0_call(
        x2d, S, row(norm_mix[0]), w0["win"], row(mla_g_cq[0]), w0["wq"], w0["gq"], row(mla_g_ckv[0]),
        w0["wkc"], w0["wkp"], w0["gk"], w0["wv"], w0["sgq"], w0["sgk"], rc, rs1, rs2)
    b3 = lambda a: a.reshape(B, S, a.shape[1])
    o_a = _mla_call(b3(qm), b3(km), b3(vm))
    o_b = _swa_call(swa_sinks[0], b3(qs), b3(ks), b3(vs), _swa_bias(t5_bias))
    o0 = jnp.concatenate([o_a, o_b], axis=-1).reshape(T, -1)
    x2d, xnt = _out_call(x2d, o0, w0["wout"], row(norm_ffn[0]))
    x2d = _peer_layer(x2d, xnt, peer_w_query[0], peer_sub_keys[0], peer_u[0], peer_v[0])

    gq2 = jnp.tile(ca_g_q[0], 2).reshape(1, LANES)
    gk2 = jnp.tile(ca_g_k[0], 2).reshape(1, LANES)
    qc, kc, vc = _in1_call(x2d, row(norm_mix[1]), od_w_in[0].astype(BF16), gq2, gk2)
    o_c = _ca_call(b3(qc), b3(kc), b3(vc), _ca_bias(ca_rel_bias[0]))
    x2d, xnt = _out_call(x2d, o_c.reshape(T, -1), od_w_out[0].astype(BF16), row(norm_ffn[1]))
    x2d = _peer_layer(x2d, xnt, peer_w_query[1], peer_sub_keys[1], peer_u[1], peer_v[1])
    return x2d.reshape(B, S, D)
```

```python
import math

import jax
import jax.numpy as jnp
from jax import lax
from jax.experimental import pallas as pl
from jax.experimental.pallas import tpu as pltpu

F32 = jnp.float32
BF16 = jnp.bfloat16

D_MODEL = 1024
CHUNK = 64
EPS = 1e-6
NEG = -1e30
LANES = 128
HALF_LANES = LANES // 2

T5_BUCKETS = 32
T5_MAX_DIST = 128

MLA_HEADS = 8
MLA_NOPE = 64
MLA_ROPE = 32
MLA_V = 64
MLA_QK = MLA_NOPE + MLA_ROPE
MLA_Q_RANK = 384
MLA_KV_RANK = 256
ROPE_THETA = 10000.0

SWA_HEADS = 8
SWA_KV_HEADS = 2
SWA_GROUP = SWA_HEADS // SWA_KV_HEADS
SWA_HEAD_DIM = 64
SWA_BAND = 2

CA_HEADS = 16
CA_HEAD_DIM = 64
CA_LEFT_CHUNKS = 8
CA_REL_PAST = 256

PEER_HEADS = 8
PEER_N_KEYS = 128
PEER_HALF = 128
PEER_TOPK = 16

VMEM_LIMIT = 56 * 1024 * 1024

TM_IN0 = 256
TM_PROJ = 512
TQ_MLA = 512
TQ_BAND = 128
TQ_CA = 256
TQ_PEER = 512
EB_PEER = 2048
TS_PEER = 256
QR_PEER = 16
KEY_GROUP = 8

GELU_C0 = math.sqrt(2.0 / math.pi)
GELU_C1 = GELU_C0 * 0.044715


def _cparams(sem):
    return pltpu.CompilerParams(dimension_semantics=sem, vmem_limit_bytes=VMEM_LIMIT)


def _rms(x, g):
    ms = jnp.mean(x * x, axis=-1, keepdims=True)
    return x * lax.rsqrt(ms + EPS) * g


def _rms_two_heads(x, g2, scale):
    lane = lax.broadcasted_iota(jnp.int32, x.shape, 1)
    lo_mask = lane < HALF_LANES
    s = x * x
    lo = jnp.sum(jnp.where(lo_mask, s, 0.0), axis=-1, keepdims=True)
    hi = jnp.sum(jnp.where(lo_mask, 0.0, s), axis=-1, keepdims=True)
    r = jnp.where(lo_mask, lax.rsqrt(lo * (1.0 / HALF_LANES) + EPS), lax.rsqrt(hi * (1.0 / HALF_LANES) + EPS))
    return x * r * (g2 * scale)


def _dot_nt(a, b):
    return lax.dot_general(a, b, (((1,), (1,)), ((), ())), preferred_element_type=F32)


COL_CQ = 0
COL_CKV = COL_CQ + MLA_Q_RANK
COL_SQ = COL_CKV + MLA_KV_RANK
COL_SK = COL_SQ + SWA_HEADS * SWA_HEAD_DIM
COL_SV = COL_SK + SWA_KV_HEADS * SWA_HEAD_DIM
COL_KPE = COL_SV + SWA_KV_HEADS * SWA_HEAD_DIM
COL_END = COL_KPE + LANES
def _in0_kernel(x_ref, gmix_ref, win_ref, gcq_ref, wq_ref, gq_ref, gckv_ref, wkc_ref, wkp_ref, gk_ref,
                wv_ref, sgq_ref, sgk_ref, ind_ref, rc_ref, rs1_ref, rs2_ref,
                qm_ref, km_ref, vm_ref, qs_ref, ks_ref, vs_ref):
    xn = _rms(x_ref[...], gmix_ref[...]).astype(BF16)
    p = jnp.dot(xn, win_ref[...], preferred_element_type=F32)
    cq = _rms(p[:, COL_CQ:COL_CKV], gcq_ref[...]).astype(BF16)
    ckv = _rms(p[:, COL_CKV:COL_SQ], gckv_ref[...]).astype(BF16)
    kpe = p[:, COL_KPE:COL_END].astype(BF16)
    q = jnp.dot(cq, wq_ref[...], preferred_element_type=F32)
    k = (jnp.dot(ckv, wkc_ref[...], preferred_element_type=F32)
         + jnp.dot(kpe, wkp_ref[...], preferred_element_type=F32))
    vm_ref[...] = jnp.dot(ckv, wv_ref[...], preferred_element_type=F32).astype(BF16)

    rc, rs1, rs2 = rc_ref[...], rs1_ref[...], rs2_ref[...]
    msq = jnp.dot((q * q).astype(BF16), ind_ref[...], preferred_element_type=F32) * (1.0 / MLA_QK)
    msk = jnp.dot((k * k).astype(BF16), ind_ref[...], preferred_element_type=F32) * (1.0 / MLA_QK)

    def head_norm_rope(t, ms, g, scale):
        tn = t * lax.rsqrt(ms + EPS) * g
        tr = tn * rc + pltpu.roll(tn, LANES - MLA_ROPE // 2, 1) * rs1 + pltpu.roll(tn, MLA_ROPE // 2, 1) * rs2
        return (tr * scale).astype(BF16)

    for h in range(MLA_HEADS):
        sl = slice(h * LANES, (h + 1) * LANES)
        qm_ref[:, sl] = head_norm_rope(q[:, sl], msq[:, sl], gq_ref[...], MLA_QK ** -0.5)
        km_ref[:, sl] = head_norm_rope(k[:, sl], msk[:, sl], gk_ref[...], 1.0)

    for m in range(SWA_HEADS // 2):
        sl = slice(COL_SQ + m * LANES, COL_SQ + (m + 1) * LANES)
        qs_ref[:, m * LANES:(m + 1) * LANES] = _rms_two_heads(p[:, sl], sgq_ref[...], SWA_HEAD_DIM ** -0.5).astype(BF16)
    ks_ref[...] = _rms_two_heads(p[:, COL_SK:COL_SV], sgk_ref[...], 1.0).astype(BF16)
    vs_ref[...] = p[:, COL_SV:COL_KPE].astype(BF16)


def _in0_call(x2d, seq, gmix, win, gcq, wq, gq, gckv, wkc, wkp, gk, wv, sgq, sgk, rc, rs1, rs2):
    T = x2d.shape[0]
    ind = jnp.kron(jnp.eye(MLA_HEADS, dtype=F32), jnp.ones((LANES, LANES), F32)).astype(BF16)
    tm = TM_IN0
    nblk_seq = seq // tm
    row = lambda i: (i, 0)
    const = lambda i: (0, 0)
    pos = lambda i: (i % nblk_seq, 0)
    full = lambda a: pl.BlockSpec(a.shape, const)
    out_shapes = (
        jax.ShapeDtypeStruct((T, MLA_HEADS * LANES), BF16),
        jax.ShapeDtypeStruct((T, MLA_HEADS * LANES), BF16),
        jax.ShapeDtypeStruct((T, MLA_HEADS * MLA_V), BF16),
        jax.ShapeDtypeStruct((T, SWA_HEADS * SWA_HEAD_DIM), BF16),
        jax.ShapeDtypeStruct((T, SWA_KV_HEADS * SWA_HEAD_DIM), BF16),
        jax.ShapeDtypeStruct((T, SWA_KV_HEADS * SWA_HEAD_DIM), BF16),
    )
    return pl.pallas_call(
        _in0_kernel,
        grid=(T // tm,),
        in_specs=[pl.BlockSpec((tm, D_MODEL), row), full(gmix), full(win), full(gcq), full(wq), full(gq),
                  full(gckv), full(wkc), full(wkp), full(gk), full(wv), full(sgq), full(sgk), full(ind),
                  pl.BlockSpec((tm, LANES), pos), pl.BlockSpec((tm, LANES), pos), pl.BlockSpec((tm, LANES), pos)],
        out_specs=tuple(pl.BlockSpec((tm, s.shape[1]), row) for s in out_shapes),
        out_shape=out_shapes,
        compiler_params=_cparams(("parallel",)),
        name="layer0_in_proj",
    )(x2d, gmix, win, gcq, wq, gq, gckv, wkc, wkp, gk, wv, sgq, sgk, ind, rc, rs1, rs2)


def _mla_kernel(q_ref, k_ref, v_ref, o_ref, m_scr, l_scr, acc_scr):
    i = pl.program_id(1)
    tq = TQ_MLA
    rows = lax.broadcasted_iota(jnp.int32, (tq, tq), 0)
    cols = lax.broadcasted_iota(jnp.int32, (tq, tq), 1)
    diag_ok = (cols // CHUNK) <= (rows // CHUNK)
    lane = lax.broadcasted_iota(jnp.int32, (tq, LANES), 1)
    m_scr[...] = jnp.full(m_scr.shape, NEG, F32)
    l_scr[...] = jnp.zeros_like(l_scr)
    acc_scr[...] = jnp.zeros_like(acc_scr)

    def body(j, _):
        start = pl.multiple_of(j * tq, tq)
        ok = jnp.logical_or(j < i, diag_ok)

        def scores(h):
            return _dot_nt(q_ref[0, :, h * LANES:(h + 1) * LANES],
                           k_ref[0, pl.ds(start, tq), h * LANES:(h + 1) * LANES])

        s_next = scores(0)
        for h in range(MLA_HEADS):
            s = jnp.where(ok, s_next, NEG)
            if h + 1 < MLA_HEADS:
                s_next = scores(h + 1)
            vb = v_ref[0, pl.ds(start, tq), (h // 2) * LANES:(h // 2 + 1) * LANES]
            m_old = m_scr[h]
            m_new = jnp.maximum(m_old, jnp.max(s, axis=-1, keepdims=True))
            alpha = jnp.exp(m_old - m_new)
            p = jnp.exp(s - m_new)
            l_scr[h] = alpha * l_scr[h] + jnp.sum(p, axis=-1, keepdims=True)
            acc_scr[h] = alpha * acc_scr[h] + jnp.dot(p.astype(BF16), vb, preferred_element_type=F32)
            m_scr[h] = m_new
        return 0

    lax.fori_loop(0, i + 1, body, 0)
    for hp in range(MLA_HEADS // 2):
        lo = acc_scr[2 * hp] / l_scr[2 * hp]
        hi = acc_scr[2 * hp + 1] / l_scr[2 * hp + 1]
        o_ref[0, :, hp * LANES:(hp + 1) * LANES] = jnp.where(lane < HALF_LANES, lo, hi).astype(BF16)


def _mla_call(q, k, v):
    B, S, _ = q.shape
    tq = TQ_MLA
    return pl.pallas_call(
        _mla_kernel,
        grid=(B, S // tq),
        in_specs=[pl.BlockSpec((1, tq, q.shape[2]), lambda b, i: (b, i, 0)),
                  pl.BlockSpec((1, S, k.shape[2]), lambda b, i: (b, 0, 0)),
                  pl.BlockSpec((1, S, v.shape[2]), lambda b, i: (b, 0, 0))],
        out_specs=pl.BlockSpec((1, tq, v.shape[2]), lambda b, i: (b, i, 0)),
        out_shape=jax.ShapeDtypeStruct((B, S, v.shape[2]), BF16),
        scratch_shapes=[pltpu.VMEM((MLA_HEADS, tq, 1), F32), pltpu.VMEM((MLA_HEADS, tq, 1), F32),
                        pltpu.VMEM((MLA_HEADS, tq, LANES), F32)],
        compiler_params=_cparams(("parallel", "arbitrary")),
        name="latent_attention",
    )(q, k, v)


def _swa_kernel(sink_ref, q_ref, kp_ref, kc_ref, vp_ref, vc_ref, bias_ref, o_ref):
    i = pl.program_id(1)
    tq = TQ_BAND
    kk = jnp.concatenate([kp_ref[0], kc_ref[0]], axis=0)
    vv = jnp.concatenate([vp_ref[0], vc_ref[0]], axis=0)
    lane = lax.broadcasted_iota(jnp.int32, (tq, LANES), 1)
    col = lax.broadcasted_iota(jnp.int32, (tq, 2 * tq), 1)
    key_ok = jnp.logical_or(col >= tq, i > 0)
    for m in range(SWA_HEADS // 2):
        qq = q_ref[0, :, m * LANES:(m + 1) * LANES]
        zero = jnp.zeros_like(qq)
        outs = []
        for hh in range(2):
            h = m + hh * SWA_GROUP
            qx = jnp.where(lane < HALF_LANES, qq, zero) if hh == 0 else jnp.where(lane < HALF_LANES, zero, qq)
            s = _dot_nt(qx, kk) + bias_ref[h]
            s = jnp.where(key_ok, s, NEG)
            sink = sink_ref[h]
            mx = jnp.maximum(jnp.max(s, axis=-1, keepdims=True), sink)
            p = jnp.exp(s - mx)
            den = jnp.sum(p, axis=-1, keepdims=True) + jnp.exp(sink - mx)
            o = jnp.dot(p.astype(BF16), vv, preferred_element_type=F32)
            outs.append(o / den)
        o_ref[0, :, m * LANES:(m + 1) * LANES] = jnp.where(lane < HALF_LANES, outs[0], outs[1]).astype(BF16)


def _swa_call(sinks, q, k, v, bias):
    B, S, _ = q.shape
    tq = TQ_BAND
    prev = lambda b, i: (b, jnp.maximum(i - 1, 0), 0)
    cur = lambda b, i: (b, i, 0)
    return pl.pallas_call(
        _swa_kernel,
        grid=(B, S // tq),
        in_specs=[pl.BlockSpec(memory_space=pltpu.SMEM),
                  pl.BlockSpec((1, tq, q.shape[2]), cur),
                  pl.BlockSpec((1, tq, LANES), prev), pl.BlockSpec((1, tq, LANES), cur),
                  pl.BlockSpec((1, tq, LANES), prev), pl.BlockSpec((1, tq, LANES), cur),
                  pl.BlockSpec(bias.shape, lambda b, i: (0, 0, 0))],
        out_specs=pl.BlockSpec((1, tq, q.shape[2]), cur),
        out_shape=jax.ShapeDtypeStruct(q.shape, BF16),
        compiler_params=_cparams(("parallel", "arbitrary")),
        name="sliding_window_attention",
    )(sinks, q, k, k, v, v, bias)


CA_KBLOCKS = CA_LEFT_CHUNKS * CHUNK // TQ_CA + 1


def _ca_kernel(q_ref, k_ref, v_ref, bias_ref, o_ref):
    i = pl.program_id(1)
    tq = TQ_CA
    nk = CA_KBLOCKS * tq
    lane = lax.broadcasted_iota(jnp.int32, (tq, LANES), 1)
    col = lax.broadcasted_iota(jnp.int32, (tq, nk), 1)
    key_ok = (col // tq) >= (CA_KBLOCKS - 1 - i)
    starts = [pl.multiple_of(jnp.maximum(i - (CA_KBLOCKS - 1) + j, 0) * tq, tq) for j in range(CA_KBLOCKS)]

    def scores(h):
        sl = slice((h // 2) * LANES, (h // 2 + 1) * LANES)
        kk = jnp.concatenate([k_ref[0, pl.ds(st, tq), sl] for st in starts], axis=0)
        qq = q_ref[0, :, sl]
        zero = jnp.zeros_like(qq)
        qx = jnp.where(lane < HALF_LANES, qq, zero) if h % 2 == 0 else jnp.where(lane < HALF_LANES, zero, qq)
        return _dot_nt(qx, kk)

    s_next = scores(0)
    outs = []
    for h in range(CA_HEADS):
        sl = slice((h // 2) * LANES, (h // 2 + 1) * LANES)
        s = jnp.where(key_ok, s_next + bias_ref[h], NEG)
        if h + 1 < CA_HEADS:
            s_next = scores(h + 1)
        vv = jnp.concatenate([v_ref[0, pl.ds(st, tq), sl] for st in starts], axis=0)
        mx = jnp.max(s, axis=-1, keepdims=True)
        p = jnp.exp(s - mx)
        den = jnp.sum(p, axis=-1, keepdims=True)
        o = jnp.dot(p.astype(BF16), vv, preferred_element_type=F32)
        outs.append(o / den)
        if h % 2 == 1:
            o_ref[0, :, sl] = jnp.where(lane < HALF_LANES, outs[0], outs[1]).astype(BF16)
            outs = []


def _ca_call(q, k, v, bias):
    B, S, W = q.shape
    tq = TQ_CA
    return pl.pallas_call(
        _ca_kernel,
        grid=(B, S // tq),
        in_specs=[pl.BlockSpec((1, tq, W), lambda b, i: (b, i, 0)),
                  pl.BlockSpec((1, S, W), lambda b, i: (b, 0, 0)),
                  pl.BlockSpec((1, S, W), lambda b, i: (b, 0, 0)),
                  pl.BlockSpec(bias.shape, lambda b, i: (0, 0, 0))],
        out_specs=pl.BlockSpec((1, tq, W), lambda b, i: (b, i, 0)),
        out_shape=jax.ShapeDtypeStruct(q.shape, BF16),
        compiler_params=_cparams(("parallel", "arbitrary")),
        name="chunk_attention",
    )(q, k, v, bias)


def _in1_kernel(x_ref, gmix_ref, win_ref, gq_ref, gk_ref, q_ref, k_ref, v_ref):
    xn = _rms(x_ref[...], gmix_ref[...]).astype(BF16)
    p = jnp.dot(xn, win_ref[...], preferred_element_type=F32)
    w = CA_HEADS * CA_HEAD_DIM
    for m in range(CA_HEADS // 2):
        sl = slice(m * LANES, (m + 1) * LANES)
        q_ref[:, sl] = _rms_two_heads(p[:, m * LANES:(m + 1) * LANES], gq_ref[...], CA_HEAD_DIM ** -0.5).astype(BF16)
        k_ref[:, sl] = _rms_two_heads(p[:, w + m * LANES:w + (m + 1) * LANES], gk_ref[...], 1.0).astype(BF16)
    v_ref[...] = p[:, 2 * w:3 * w].astype(BF16)


def _in1_call(x2d, gmix, win, gq, gk):
    T = x2d.shape[0]
    tm = TM_PROJ
    w = CA_HEADS * CA_HEAD_DIM
    row = lambda i: (i, 0)
    const = lambda i: (0, 0)
    full = lambda a: pl.BlockSpec(a.shape, const)
    out_shapes = tuple(jax.ShapeDtypeStruct((T, w), BF16) for _ in range(3))
    return pl.pallas_call(
        _in1_kernel,
        grid=(T // tm,),
        in_specs=[pl.BlockSpec((tm, D_MODEL), row), full(gmix), full(win), full(gq), full(gk)],
        out_specs=tuple(pl.BlockSpec((tm, w), row) for _ in range(3)),
        out_shape=out_shapes,
        compiler_params=_cparams(("parallel",)),
        name="layer1_in_proj",
    )(x2d, gmix, win, gq, gk)


def _out_kernel(x_ref, o_ref, w_ref, g_ref, xo_ref, xnt_ref):
    y = x_ref[...] + jnp.dot(o_ref[...], w_ref[...], preferred_element_type=F32)
    xo_ref[...] = y
    xnt_ref[...] = _rms(y, g_ref[...]).T.astype(BF16)


def _out_call(x2d, o2d, w, g):
    T = x2d.shape[0]
    tm = TM_PROJ
    row = lambda i: (i, 0)
    const = lambda i: (0, 0)
    return pl.pallas_call(
        _out_kernel,
        grid=(T // tm,),
        in_specs=[pl.BlockSpec((tm, D_MODEL), row), pl.BlockSpec((tm, o2d.shape[1]), row),
                  pl.BlockSpec(w.shape, const), pl.BlockSpec(g.shape, const)],
        out_specs=(pl.BlockSpec((tm, D_MODEL), row), pl.BlockSpec((D_MODEL, tm), lambda i: (0, i))),
        out_shape=(jax.ShapeDtypeStruct((T, D_MODEL), F32), jax.ShapeDtypeStruct((D_MODEL, T), BF16)),
        compiler_params=_cparams(("parallel",)),
        name="out_proj_residual",
    )(x2d, o2d, w, g)


NTB_PEER = TQ_PEER // LANES
I1_PER_STEP = EB_PEER // PEER_N_KEYS


SUBLANES = 8
BIG = 3.0e38

SORT16 = (
    (0, 13), (1, 12), (2, 15), (3, 14), (4, 8), (5, 6), (7, 11), (9, 10),
    (0, 5), (1, 7), (2, 9), (3, 4), (6, 13), (8, 14), (10, 15), (11, 12),
    (0, 1), (2, 3), (4, 5), (6, 8), (7, 9), (10, 11), (12, 13), (14, 15),
    (0, 2), (1, 3), (4, 10), (5, 11), (6, 7), (8, 9), (12, 14), (13, 15),
    (1, 2), (3, 12), (4, 6), (5, 7), (8, 10), (9, 11), (13, 14),
    (1, 4), (2, 6), (5, 8), (7, 10), (9, 13), (11, 14),
    (2, 4), (3, 6), (9, 12), (11, 13),
    (3, 5), (6, 8), (7, 9), (10, 12),
    (3, 4), (5, 6), (7, 8), (9, 10), (11, 12),
    (6, 7), (8, 9),
)
BITONIC16 = tuple((r, r + st) for st in (8, 4, 2, 1) for r in range(16) if (r // st) % 2 == 0)


def _compare_exchange(v, i, j):
    hi, lo = jnp.maximum(v[i], v[j]), jnp.minimum(v[i], v[j])
    v[i], v[j] = hi, lo


def _merge_across_sublanes(v):
    for shift in (4, 2, 1):
        y = [pltpu.roll(t, shift, 0) for t in v]
        v = [jnp.maximum(v[r], y[PEER_TOPK - 1 - r]) for r in range(PEER_TOPK)]
        for i, j in BITONIC16:
            _compare_exchange(v, i, j)
    return v


def _top16(x):
    v = [x[SUBLANES * i:SUBLANES * (i + 1), :] for i in range(PEER_N_KEYS // SUBLANES)]
    for i, j in SORT16:
        _compare_exchange(v, i, j)
    return _merge_across_sublanes(v)


def _route_tile(s1, s2):
    ta = _top16(s1)
    tb = _top16(s2)
    sub = lax.broadcasted_iota(jnp.int32, (SUBLANES, LANES), 0)
    a8 = ta[SUBLANES - 1]
    for j in range(SUBLANES - 2, -1, -1):
        a8 = jnp.where(sub == j, ta[j], a8)
    cl = []
    for k in range(PEER_TOPK):
        nj = PEER_TOPK // (k + 1)
        l = a8 + tb[k]
        cl.append(l if nj >= SUBLANES else jnp.where(sub < nj, l, NEG))
    m = _merge_across_sublanes(list(cl))
    ex = [ta[SUBLANES + r] + tb[0] for r in range(SUBLANES)]
    top = m[:SUBLANES] + [jnp.maximum(m[r], ex[PEER_TOPK - 1 - r]) for r in range(SUBLANES, PEER_TOPK)]
    for i, j in BITONIC16:
        _compare_exchange(top, i, j)
    tau = top[PEER_TOPK - 1]
    z = jnp.exp(top[0] - top[0])
    for r in range(1, PEER_TOPK):
        z = z + jnp.exp(top[r] - top[0])
    rz = 0.5 / z

    e2k = [jnp.exp(tb[k] - tb[0]) for k in range(PEER_TOPK)]
    e1top = jnp.exp(a8 - ta[0]) * rz
    thr = jnp.full((SUBLANES, LANES), BIG, F32)
    for k in range(PEER_TOPK):
        thr = jnp.minimum(thr, jnp.where(cl[k] >= tau, e1top * e2k[k], BIG))
    for r in range(SUBLANES):
        e1x = jnp.exp(ta[SUBLANES + r] - ta[0]) * rz
        thr = jnp.minimum(thr, jnp.where(ex[r] >= tau, e1x * e2k[0], BIG))
    for shift in (4, 2, 1):
        thr = jnp.minimum(thr, pltpu.roll(thr, shift, 0))
    n = PEER_N_KEYS // SUBLANES
    e1 = jnp.concatenate([jnp.exp(s1[SUBLANES * i:SUBLANES * (i + 1), :] - ta[0]) * rz for i in range(n)], axis=0)
    e2 = jnp.concatenate([jnp.exp(s2[SUBLANES * i:SUBLANES * (i + 1), :] - tb[0]) for i in range(n)], axis=0)
    return e1, e2, thr


def _peer_kernel(x_ref, xnt_ref, wqt_ref, sk_ref, u_ref, vt_ref, o_ref,
                 q_scr, sc_scr, e1_scr, e2_scr, thr_scr, a_scr, g_scr, acc_scr):
    e = pl.program_id(1)

    @pl.when(e == 0)
    def _route():
        q_scr[...] = jnp.dot(wqt_ref[...], xnt_ref[...], preferred_element_type=F32).astype(BF16)
        acc_scr[...] = jnp.zeros_like(acc_scr)

        def per_head(h, _):
            for p in range(2):
                hp = h * 2 + p
                qs = q_scr[pl.ds(pl.multiple_of(hp * PEER_HALF, PEER_HALF), PEER_HALF), :]
                sc = jnp.dot(sk_ref[hp], qs, preferred_element_type=F32)
                for tb in range(NTB_PEER):
                    sc_scr[p, tb] = sc[:, tb * LANES:(tb + 1) * LANES]

            def per_tile(tb, _):
                e1, e2, thr = _route_tile(sc_scr[0, tb], sc_scr[1, tb])
                e1_scr[h, tb] = e1
                e2_scr[h, tb] = e2
                thr_scr[h, tb] = thr
                return 0

            lax.fori_loop(0, NTB_PEER, per_tile, 0)
            return 0

        lax.fori_loop(0, PEER_HEADS, per_head, 0)

    n_sub = TQ_PEER // TS_PEER

    def route_weights(s, key0):
        for tb2 in range(TS_PEER // LANES):
            tb = s * (TS_PEER // LANES) + tb2
            ts = slice(tb2 * LANES, (tb2 + 1) * LANES)
            for q in range(PEER_N_KEYS // QR_PEER):
                wsum = [jnp.zeros((QR_PEER, LANES), F32) for _ in range(KEY_GROUP)]
                for h in range(PEER_HEADS):
                    e2q = jnp.maximum(e2_scr[h, tb, q * QR_PEER:(q + 1) * QR_PEER, :], 0.0)
                    thr = thr_scr[h, tb, 0:1, :]
                    for r in range(KEY_GROUP):
                        w = e2q * e1_scr[h, tb, pl.ds(e * I1_PER_STEP + key0 + r, 1), :]
                        wsum[r] = wsum[r] + jnp.where(w >= thr, w, 0.0)
                for r in range(KEY_GROUP):
                    row0 = (key0 + r) * PEER_N_KEYS + q * QR_PEER
                    rows = slice(row0, row0 + QR_PEER)
                    at = a_scr[s, rows, ts]
                    g = wsum[r] * at * (1.0 + jnp.tanh(at * (GELU_C0 + GELU_C1 * (at * at))))
                    g_scr[s, rows, ts] = g.astype(BF16)

    for s in range(n_sub):
        a_scr[s] = jnp.dot(u_ref[...], xnt_ref[:, s * TS_PEER:(s + 1) * TS_PEER],
                           preferred_element_type=F32)
    for s in range(n_sub):
        for key0 in range(0, I1_PER_STEP, KEY_GROUP):
            route_weights(s, key0)
        acc_scr[:, s * TS_PEER:(s + 1) * TS_PEER] += jnp.dot(vt_ref[...], g_scr[s], preferred_element_type=F32)

    @pl.when(e == pl.num_programs(1) - 1)
    def _finish():
        o_ref[...] = x_ref[...] + acc_scr[...].T


def _peer_call(x2d, xnt, wqt, sk, u, vt):
    T = x2d.shape[0]
    tq, eb = TQ_PEER, EB_PEER
    n_exp = u.shape[0]
    return pl.pallas_call(
        _peer_kernel,
        grid=(T // tq, n_exp // eb),
        in_specs=[pl.BlockSpec((tq, D_MODEL), lambda t, e: (t, 0)),
                  pl.BlockSpec((D_MODEL, tq), lambda t, e: (0, t)),
                  pl.BlockSpec(wqt.shape, lambda t, e: (0, 0)),
                  pl.BlockSpec(sk.shape, lambda t, e: (0, 0, 0)),
                  pl.BlockSpec((eb, D_MODEL), lambda t, e: (e, 0)),
                  pl.BlockSpec((D_MODEL, eb), lambda t, e: (0, e))],
        out_specs=pl.BlockSpec((tq, D_MODEL), lambda t, e: (t, 0)),
        out_shape=jax.ShapeDtypeStruct((T, D_MODEL), F32),
        scratch_shapes=[
            pltpu.VMEM((2 * PEER_HEADS * PEER_HALF, tq), BF16),
            pltpu.VMEM((2, NTB_PEER, PEER_N_KEYS, LANES), F32),
            pltpu.VMEM((PEER_HEADS, NTB_PEER, PEER_N_KEYS, LANES), F32),
            pltpu.VMEM((PEER_HEADS, NTB_PEER, PEER_N_KEYS, LANES), F32),
            pltpu.VMEM((PEER_HEADS, NTB_PEER, SUBLANES, LANES), F32),
            pltpu.VMEM((tq // TS_PEER, eb, TS_PEER), F32),
            pltpu.VMEM((tq // TS_PEER, eb, TS_PEER), BF16),
            pltpu.VMEM((D_MODEL, tq), F32),
        ],
        compiler_params=_cparams(("parallel", "arbitrary")),
        name="peer_dense",
    )(x2d, xnt, wqt, sk, u, vt)


def _t5_bucket(rel):
    nb = T5_BUCKETS // 2
    max_exact = nb // 2
    ret = jnp.where(rel > 0, nb, 0)
    n = jnp.abs(rel)
    large = max_exact + (jnp.log(jnp.maximum(n, 1).astype(F32) / max_exact)
                         / math.log(T5_MAX_DIST / max_exact) * (nb - max_exact)).astype(jnp.int32)
    large = jnp.minimum(large, nb - 1)
    return ret + jnp.where(n < max_exact, n, large)


def _toeplitz(f, n_rows, n_cols):
    heads, width = f.shape
    period = jnp.concatenate([f[:, n_rows - 1:], jnp.zeros((heads, 1), f.dtype), f[:, :n_rows - 1]], axis=1)
    stream = jnp.tile(period, (1, n_rows))[:, :n_rows * width]
    return stream.reshape(heads, n_rows, width)[:, :, :n_cols]


def _swa_bias(t5_table):
    tq = TQ_BAND
    r = jnp.arange(tq)[:, None]
    c = jnp.arange(2 * tq)[None, :]
    rel_line = jnp.arange(3 * tq - 1) - (2 * tq - 1)
    bias = _toeplitz(t5_table[_t5_bucket(rel_line)].astype(F32).T, tq, 2 * tq)
    qc = (tq + r) // CHUNK
    kc = c // CHUNK
    vis = jnp.logical_and(kc >= qc - SWA_BAND, kc <= qc)
    return jnp.where(vis[None], bias, NEG)


def _ca_bias(rel_table):
    tq = TQ_CA
    left = CA_LEFT_CHUNKS * CHUNK
    r = jnp.arange(tq)[:, None]
    c = jnp.arange(left + tq)[None, :]
    rel_line = jnp.arange(left + 2 * tq - 1) - (left + tq - 1)
    idx = jnp.clip(rel_line, -CA_REL_PAST, CHUNK - 1) + CA_REL_PAST
    bias = _toeplitz(rel_table[:, idx].astype(F32), tq, left + tq)
    qc = (left + r) // CHUNK
    kc = c // CHUNK
    vis = jnp.logical_and(kc >= qc - CA_LEFT_CHUNKS, kc <= qc)
    return jnp.where(vis[None], bias, NEG)


def _rope_tables(seq):
    half = MLA_ROPE // 2
    inv = ROPE_THETA ** (-jnp.arange(0, MLA_ROPE, 2, dtype=F32) / MLA_ROPE)
    ang = jnp.arange(seq, dtype=F32)[:, None] * inv[None, :]
    cos, sin = jnp.cos(ang), jnp.sin(ang)
    z = lambda n: jnp.zeros((seq, n), F32)
    tail = LANES - MLA_QK
    rc = jnp.concatenate([jnp.ones((seq, MLA_NOPE), F32), cos, cos, z(tail)], axis=1)
    rs1 = jnp.concatenate([z(MLA_NOPE), -sin, z(half), z(tail)], axis=1)
    rs2 = jnp.concatenate([z(MLA_NOPE), z(half), sin, z(tail)], axis=1)
    return rc, rs1, rs2


SWA_HEAD_ORDER = tuple(h for m in range(SWA_GROUP) for h in (m, m + SWA_GROUP))


def _layer0_weights(ev_w_in, ev_w_out, mla_w_uq, mla_w_ukv, mla_g_q, mla_g_k, swa_g_q, swa_g_k):
    pad_head = LANES - MLA_QK
    mla_in = MLA_Q_RANK + MLA_KV_RANK + MLA_ROPE
    nq = SWA_HEADS * SWA_HEAD_DIM
    nkv = SWA_KV_HEADS * SWA_HEAD_DIM
    order = jnp.asarray(SWA_HEAD_ORDER)
    w_cq_ckv = ev_w_in[:, :MLA_Q_RANK + MLA_KV_RANK]
    w_kpe = ev_w_in[:, MLA_Q_RANK + MLA_KV_RANK:mla_in]
    w_sq = ev_w_in[:, mla_in:mla_in + nq].reshape(D_MODEL, SWA_HEADS, SWA_HEAD_DIM)[:, order].reshape(D_MODEL, nq)
    w_skv = ev_w_in[:, mla_in + nq:mla_in + nq + 2 * nkv]
    win = jnp.concatenate([w_cq_ckv, w_sq, w_skv, w_kpe, jnp.zeros((D_MODEL, LANES - MLA_ROPE), F32)], axis=1)
    wq = jnp.pad(mla_w_uq, ((0, 0), (0, 0), (0, pad_head))).reshape(MLA_Q_RANK, MLA_HEADS * LANES)
    wkc = jnp.pad(mla_w_ukv[:, :, :MLA_NOPE], ((0, 0), (0, 0), (0, LANES - MLA_NOPE))).reshape(MLA_KV_RANK, MLA_HEADS * LANES)
    place = jnp.pad(jnp.eye(MLA_ROPE, dtype=F32), ((0, LANES - MLA_ROPE), (MLA_NOPE, pad_head)))
    wkp = jnp.tile(place, (1, MLA_HEADS))
    wv = mla_w_ukv[:, :, MLA_NOPE:].reshape(MLA_KV_RANK, MLA_HEADS * MLA_V)
    gq = jnp.pad(mla_g_q, (0, pad_head)).reshape(1, LANES)
    gk = jnp.pad(mla_g_k, (0, pad_head)).reshape(1, LANES)
    sgq = jnp.tile(swa_g_q, 2).reshape(1, LANES)
    sgk = jnp.tile(swa_g_k, 2).reshape(1, LANES)
    n_a = MLA_HEADS * MLA_V
    w_out_b = ev_w_out[n_a:].reshape(SWA_HEADS, SWA_HEAD_DIM, D_MODEL)[order].reshape(nq, D_MODEL)
    wout = jnp.concatenate([ev_w_out[:n_a], w_out_b], axis=0)
    return dict(win=win.astype(BF16), wq=wq.astype(BF16), wkc=wkc.astype(BF16), wkp=wkp.astype(BF16),
                wv=wv.astype(BF16), gq=gq, gk=gk, sgq=sgq, sgk=sgk, wout=wout.astype(BF16))


def _peer_layer(x2d, xnt, w_query, sub_keys, u_emb, v_emb):
    wqt = w_query.reshape(D_MODEL, 2 * PEER_HEADS * PEER_HALF).T.astype(BF16)
    sk = sub_keys.reshape(2 * PEER_HEADS, PEER_N_KEYS, PEER_HALF).astype(BF16)
    return _peer_call(x2d, xnt, wqt, sk, u_emb.astype(BF16), v_emb.T.astype(BF16))


def kernel(x, t5_bias, norm_mix, norm_ffn, ev_w_in, ev_w_out, mla_g_cq, mla_w_uq, mla_g_ckv, mla_w_ukv,
           mla_g_q, mla_g_k, swa_g_q, swa_g_k, swa_sinks, od_w_in, od_w_out, ca_g_q, ca_g_k, ca_rel_bias,
           peer_w_query, peer_sub_keys, peer_u, peer_v):
    B, S, D = x.shape
    T = B * S
    x2d = x.reshape(T, D)
    row = lambda a: a.reshape(1, -1)

    w0 = _layer0_weights(ev_w_in[0], ev_w_out[0], mla_w_uq[0], mla_w_ukv[0], mla_g_q[0], mla_g_k[0],
                         swa_g_q[0], swa_g_k[0])
    rc, rs1, rs2 = _rope_tables(S)
    qm, km, vm, qs, ks, vs = _in0_call(
        x2d, S, row(norm_mix[0]), w0["win"], row(mla_g_cq[0]), w0["wq"], w0["gq"], row(mla_g_ckv[0]),
        w0["wkc"], w0["wkp"], w0["gk"], w0["wv"], w0["sgq"], w0["sgk"], rc, rs1, rs2)
    b3 = lambda a: a.reshape(B, S, a.shape[1])
    o_a = _mla_call(b3(qm), b3(km), b3(vm))
    o_b = _swa_call(swa_sinks[0], b3(qs), b3(ks), b3(vs), _swa_bias(t5_bias))
    o0 = jnp.concatenate([o_a, o_b], axis=-1).reshape(T, -1)
    x2d, xnt = _out_call(x2d, o0, w0["wout"], row(norm_ffn[0]))
    x2d = _peer_layer(x2d, xnt, peer_w_query[0], peer_sub_keys[0], peer_u[0], peer_v[0])

    gq2 = jnp.tile(ca_g_q[0], 2).reshape(1, LANES)
    gk2 = jnp.tile(ca_g_k[0], 2).reshape(1, LANES)
    qc, kc, vc = _in1_call(x2d, row(norm_mix[1]), od_w_in[0].astype(BF16), gq2, gk2)
    o_c = _ca_call(b3(qc), b3(kc), b3(vc), _ca_bias(ca_rel_bias[0]))
    x2d, xnt = _out_call(x2d, o_c.reshape(T, -1), od_w_out[0].astype(BF16), row(norm_ffn[1]))
    x2d = _peer_layer(x2d, xnt, peer_w_query[1], peer_sub_keys[1], peer_u[1], peer_v[1])
    return x2d.reshape(B, S, D)
```
